```python
import math
import jax, jax.numpy as jnp
from jax import lax
import numpy as np

D_MODEL = 1024
BATCH = 1
SEQ = 16384
DEPTH = 1
DEC_BATCH = 128
DEC_SEQ = 8
PAST_LEN = 8192
PAGE_SIZE = 128

HEAD_DIM = 64
FOX_HEADS = 8
FOX_WIDTH = FOX_HEADS * HEAD_DIM
DIFF_HEADS = 4
DIFF_QK_DIM = 64
DIFF_V_DIM = 2 * DIFF_QK_DIM
DIFF_QK_WIDTH = DIFF_HEADS * 2 * DIFF_QK_DIM
DIFF_WIDTH = DIFF_HEADS * DIFF_V_DIM
MIX_WIDTH = FOX_WIDTH + DIFF_WIDTH
N_IN = 4 * FOX_WIDTH + FOX_HEADS + 2 * DIFF_QK_WIDTH + 2 * DIFF_WIDTH
ROPE_THETA = 500000.0
ROT_DIM = DIFF_QK_DIM // 4
Q_BLOCK = 128
NORM_EPS = 1e-6

kernel_name = 'fox_diffattn_parallel_heads_step'


def rms_norm(x, g):
    xf = x.astype(jnp.float32)
    y = xf * lax.rsqrt(jnp.mean(xf * xf, axis=-1, keepdims=True) + NORM_EPS)
    return (y * g.astype(jnp.float32)).astype(x.dtype)


def apply_rope(x, pos):
    half = ROT_DIM // 2
    inv = ROPE_THETA ** (-jnp.arange(half, dtype=jnp.float32) * 2.0 / ROT_DIM)
    ang = pos.astype(jnp.float32)[:, None] * inv[None, :]
    cos = jnp.cos(ang)[:, None, None, :].astype(x.dtype)
    sin = jnp.sin(ang)[:, None, None, :].astype(x.dtype)
    x1 = x[..., :half]
    x2 = x[..., half:ROT_DIM]
    return jnp.concatenate([x1 * cos - x2 * sin, x2 * cos + x1 * sin, x[..., ROT_DIM:]], axis=-1)


def _split_columns(h):
    sizes = [FOX_WIDTH, FOX_WIDTH, FOX_WIDTH, FOX_HEADS, FOX_WIDTH,
             DIFF_QK_WIDTH, DIFF_QK_WIDTH, DIFF_WIDTH, DIFF_WIDTH]
    idx = np.cumsum(sizes)[:-1].tolist()
    return jnp.split(h, idx, axis=-1)


def _project(x, pos, g_norm, w_in, b_f, g_fq, g_fk, g_dq, g_dk):
    B, S, _ = x.shape
    xn = rms_norm(x, g_norm)
    h = jnp.einsum('bsd,dn->bsn', xn, w_in)
    fq, fk, fv, ff, fz, dq, dk, dv, dz = _split_columns(h)
    fq = rms_norm(fq.reshape(B, S, FOX_HEADS, HEAD_DIM), g_fq)
    fk = rms_norm(fk.reshape(B, S, FOX_HEADS, HEAD_DIM), g_fk)
    fv = fv.reshape(B, S, FOX_HEADS, HEAD_DIM)
    logf = jax.nn.log_sigmoid(ff.astype(jnp.float32) + b_f.astype(jnp.float32))
    dq = apply_rope(rms_norm(dq.reshape(B, S, DIFF_HEADS, 2, DIFF_QK_DIM), g_dq), pos)
    dk = apply_rope(rms_norm(dk.reshape(B, S, DIFF_HEADS, 2, DIFF_QK_DIM), g_dk), pos)
    dv = dv.reshape(B, S, DIFF_HEADS, DIFF_V_DIM)
    return fq, fk, fv, logf, fz, dq, dk, dv, dz


def _diff_lambda(lq1, lk1, lq2, lk2, lambda_init):
    f = jnp.float32
    return (jnp.exp(jnp.sum(lq1.astype(f) * lk1.astype(f)))
            - jnp.exp(jnp.sum(lq2.astype(f) * lk2.astype(f))) + lambda_init)


def fox_attend(q, k, v, cq, ck, qpos, kpos):
    s = jnp.einsum('qhd,khd->hqk', q, k).astype(jnp.float32) * (HEAD_DIM ** -0.5)
    s = s + (cq.T[:, :, None] - ck.T[:, None, :])
    s = jnp.where(kpos[None, None, :] <= qpos[None, :, None], s, -jnp.inf)
    p = jax.nn.softmax(s, axis=-1)
    return jnp.einsum('hqk,khd->qhd', p.astype(v.dtype), v)


def diff_attend(q, k, v, lam, qpos, kpos):
    s = jnp.einsum('qhmd,khmd->hmqk', q, k).astype(jnp.float32) * (DIFF_QK_DIM ** -0.5)
    s = jnp.where(kpos[None, None, None, :] <= qpos[None, None, :, None], s, -jnp.inf)
    p = jax.nn.softmax(s, axis=-1)
    a = p[:, 0] - lam * p[:, 1]
    return jnp.einsum('hqk,khd->qhd', a.astype(v.dtype), v)


def _prompt_attention(fq, fk, fv, logf, dq, dk, dv, lam):
    B, S = fq.shape[0], fq.shape[1]
    c = jnp.cumsum(logf, axis=1)
    kpos = jnp.arange(S)
    fox_b = jax.vmap(fox_attend, in_axes=(0, 0, 0, 0, 0, None, None))
    diff_b = jax.vmap(diff_attend, in_axes=(0, 0, 0, None, None, None))

    def block(i):
        start = i * Q_BLOCK
        qpos = start + jnp.arange(Q_BLOCK)
        fq_b = lax.dynamic_slice_in_dim(fq, start, Q_BLOCK, axis=1)
        cq_b = lax.dynamic_slice_in_dim(c, start, Q_BLOCK, axis=1)
        dq_b = lax.dynamic_slice_in_dim(dq, start, Q_BLOCK, axis=1)
        return fox_b(fq_b, fk, fv, cq_b, c, qpos, kpos), diff_b(dq_b, dk, dv, lam, qpos, kpos)

    of, od = lax.map(block, jnp.arange(S // Q_BLOCK))
    of = jnp.swapaxes(of, 0, 1).reshape(B, S, FOX_HEADS, HEAD_DIM)
    od = jnp.swapaxes(od, 0, 1).reshape(B, S, DIFF_HEADS, DIFF_V_DIM)
    return of, od


def _sample_attention(fq, fk, fv, logf, dq, dk, dv, lam, page_table,
                      pool_fk, pool_fv, pool_fl, pool_dk, pool_dv):
    T = fq.shape[1]
    past = page_table.shape[1] * PAGE_SIZE
    qpos = past + jnp.arange(T)
    kpos = jnp.arange(past + T)

    def one(args):
        pt, fq_b, fk_b, fv_b, lf_b, dq_b, dk_b, dv_b = args

        def gather(pool):
            return pool[pt].reshape((past,) + pool.shape[2:])

        k = jnp.concatenate([gather(pool_fk), fk_b], axis=0)
        v = jnp.concatenate([gather(pool_fv), fv_b], axis=0)
        lf = jnp.concatenate([gather(pool_fl).astype(jnp.float32), lf_b], axis=0)
        c = jnp.cumsum(lf, axis=0)
        of = fox_attend(fq_b, k, v, c[past:], c, qpos, kpos)
        kd = jnp.concatenate([gather(pool_dk), dk_b], axis=0)
        vd = jnp.concatenate([gather(pool_dv), dv_b], axis=0)
        od = diff_attend(dq_b, kd, vd, lam, qpos, kpos)
        return of, od

    return lax.map(one, (page_table, fq, fk, fv, logf, dq, dk, dv))


def _merge(x, of, od, fz, dz, g_subln, w_out, lambda_init):
    B, S, _ = x.shape
    od = rms_norm(od, g_subln) * (1.0 - lambda_init)
    o = jnp.concatenate([of.reshape(B, S, FOX_WIDTH) * jax.nn.silu(fz),
                         od.reshape(B, S, DIFF_WIDTH) * jax.nn.silu(dz)], axis=-1)
    return x + jnp.einsum('bsm,md->bsd', o, w_out)


def setup_inputs(seed: int = 0) -> dict:
    key = jax.random.key(seed)
    ks = jax.random.split(key, 24)
    n_pages = PAST_LEN // PAGE_SIZE
    n_used = DEC_BATCH * n_pages
    n_pool = n_used + (n_used + 3) // 4
    nrm = jax.random.normal
    page_table = jax.random.permutation(ks[0], n_pool)[:n_used].reshape(DEC_BATCH, n_pages).astype(jnp.int32)
    b_f = jax.random.uniform(ks[1], (DEPTH, FOX_HEADS), minval=1.0, maxval=6.0)
    cache_fox_logf = jax.nn.log_sigmoid(b_f[:, None, None, :] + nrm(ks[2], (DEPTH, n_pool, PAGE_SIZE, FOX_HEADS)))
    return {
        'x_prompt': nrm(ks[3], (BATCH, SEQ, D_MODEL)),
        'x_sample': nrm(ks[4], (DEC_BATCH, DEC_SEQ, D_MODEL)),
        'cache_fox_k': nrm(ks[5], (DEPTH, n_pool, PAGE_SIZE, FOX_HEADS, HEAD_DIM)),
        'cache_fox_v': nrm(ks[6], (DEPTH, n_pool, PAGE_SIZE, FOX_HEADS, HEAD_DIM)),
        'cache_fox_logf': cache_fox_logf,
        'cache_diff_k': nrm(ks[7], (DEPTH, n_pool, PAGE_SIZE, DIFF_HEADS, 2, DIFF_QK_DIM)),
        'cache_diff_v': nrm(ks[8], (DEPTH, n_pool, PAGE_SIZE, DIFF_HEADS, DIFF_V_DIM)),
        'page_table': page_table,
        'g_norm': 1.0 + 0.02 * nrm(ks[9], (DEPTH, D_MODEL)),
        'w_in': nrm(ks[10], (DEPTH, D_MODEL, N_IN)) * D_MODEL ** -0.5,
        'b_f': b_f,
        'g_fox_q': 1.0 + 0.02 * nrm(ks[11], (DEPTH, HEAD_DIM)),
        'g_fox_k': 1.0 + 0.02 * nrm(ks[12], (DEPTH, HEAD_DIM)),
        'g_diff_q': 1.0 + 0.02 * nrm(ks[13], (DEPTH, DIFF_QK_DIM)),
        'g_diff_k': 1.0 + 0.02 * nrm(ks[14], (DEPTH, DIFF_QK_DIM)),
        'lam_q1': 0.1 * nrm(ks[15], (DEPTH, DIFF_QK_DIM)),
        'lam_k1': 0.1 * nrm(ks[16], (DEPTH, DIFF_QK_DIM)),
        'lam_q2': 0.1 * nrm(ks[17], (DEPTH, DIFF_QK_DIM)),
        'lam_k2': 0.1 * nrm(ks[18], (DEPTH, DIFF_QK_DIM)),
        'g_subln': 1.0 + 0.02 * nrm(ks[19], (DEPTH, DIFF_V_DIM)),
        'w_out': nrm(ks[20], (DEPTH, MIX_WIDTH, D_MODEL)) * MIX_WIDTH ** -0.5,
    }


def reference(x_prompt, x_sample, cache_fox_k, cache_fox_v, cache_fox_logf, cache_diff_k, cache_diff_v,
              page_table, g_norm, w_in, b_f, g_fox_q, g_fox_k, g_diff_q, g_diff_k,
              lam_q1, lam_k1, lam_q2, lam_k2, g_subln, w_out):
    pos_p = jnp.arange(x_prompt.shape[1])
    pos_s = page_table.shape[1] * PAGE_SIZE + jnp.arange(x_sample.shape[1])
    xp, xs = x_prompt, x_sample
    fkp, fvp, flp, dkp, dvp = [], [], [], [], []
    fks, fvs, fls, dks, dvs = [], [], [], [], []
    for l in range(DEPTH):
        lambda_init = 0.8 - 0.6 * math.exp(-0.3 * l)
        lam = _diff_lambda(lam_q1[l], lam_k1[l], lam_q2[l], lam_k2[l], lambda_init)
        fq, fk, fv, lf, fz, dq, dk, dv, dz = _project(
            xp, pos_p, g_norm[l], w_in[l], b_f[l], g_fox_q[l], g_fox_k[l], g_diff_q[l], g_diff_k[l])
        of, od = _prompt_attention(fq, fk, fv, lf, dq, dk, dv, lam)
        xp = _merge(xp, of, od, fz, dz, g_subln[l], w_out[l], lambda_init)
        fkp.append(fk); fvp.append(fv); flp.append(lf); dkp.append(dk); dvp.append(dv)
        fq, fk, fv, lf, fz, dq, dk, dv, dz = _project(
            xs, pos_s, g_norm[l], w_in[l], b_f[l], g_fox_q[l], g_fox_k[l], g_diff_q[l], g_diff_k[l])
        of, od = _sample_attention(fq, fk, fv, lf, dq, dk, dv, lam, page_table,
                                   cache_fox_k[l], cache_fox_v[l], cache_fox_logf[l],
                                   cache_diff_k[l], cache_diff_v[l])
        xs = _merge(xs, of, od, fz, dz, g_subln[l], w_out[l], lambda_init)
        fks.append(fk); fvs.append(fv); fls.append(lf); dks.append(dk); dvs.append(dv)
    return (xp, xs,
            jnp.stack(fkp), jnp.stack(fvp), jnp.stack(flp), jnp.stack(dkp), jnp.stack(dvp),
            jnp.stack(fks), jnp.stack(fvs), jnp.stack(fls), jnp.stack(dks), jnp.stack(dvs))
```

```python
import functools

import jax
import jax.numpy as jnp
from jax import lax
from jax.experimental import pallas as pl
from jax.experimental.pallas import tpu as pltpu

D_MODEL = 1024
HEAD_DIM = 64
FOX_HEADS = 8
FOX_WIDTH = FOX_HEADS * HEAD_DIM
DIFF_HEADS = 4
DIFF_QK_DIM = 64
DIFF_V_DIM = 2 * DIFF_QK_DIM
DIFF_QK_WIDTH = DIFF_HEADS * 2 * DIFF_QK_DIM
DIFF_WIDTH = DIFF_HEADS * DIFF_V_DIM
PAGE_SIZE = 128
ROPE_THETA = 500000.0
ROT_DIM = DIFF_QK_DIM // 4
NORM_EPS = 1e-6
LAMBDA_INIT = 0.2

LOG2E = 1.4426950408889634
QK_SCALE = LOG2E * HEAD_DIM ** -0.5
NEG = -1e30

F32 = jnp.float32
BF16 = jnp.bfloat16

FOX_DK = 128
FOX_DVA = 80
DIFF_DVA = 144

VMEM_LIMIT = 48 * 1024 * 1024

R_FQ, R_FK, R_FV, R_FZ, R_DQ, R_DK, R_DV, R_DZ, R_FF = (i * 512 for i in range(9))
WT_ROWS = R_FF + 16


def _lane_cumsum(x):
    lane = lax.broadcasted_iota(jnp.int32, x.shape, 1)
    sh = 1
    while sh < x.shape[1]:
        x = x + jnp.where(lane >= sh, pltpu.roll(x, sh, 1), 0.0)
        sh *= 2
    return x


def _silu(z):
    return z * (1.0 / (1.0 + jnp.exp(-z)))


def _log_sigmoid(z):
    return jnp.minimum(z, 0.0) - jnp.log1p(jnp.exp(-jnp.abs(z)))


def _diff_lambda(lq1, lk1, lq2, lk2):
    a = jnp.sum(lq1[...] * lk1[...], axis=-1, keepdims=True)
    b = jnp.sum(lq2[...] * lk2[...], axis=-1, keepdims=True)
    return jnp.exp(a) - jnp.exp(b) + LAMBDA_INIT


def _proj_body(is_prompt, tm, pos_base, pos_tile, pos_lane, *refs):
    (x_ref, gn_ref, wT_ref, wdv_ref, bf_ref, gfq_ref, gfk_ref, gdq_ref, gdk_ref, inv_ref) = refs[:10]
    rest = refs[10:]
    i = pl.program_id(0)

    x = x_ref[...]
    ms = jnp.mean(x * x, axis=-1, keepdims=True)
    xn = (x * lax.rsqrt(ms + NORM_EPS)) * gn_ref[...]
    xb = xn.astype(BF16)

    def proj_t(r0, n):
        return lax.dot_general(wT_ref[r0:r0 + n, :], xb, (((1,), (1,)), ((), ())),
                               preferred_element_type=F32)

    def head_norm(h_t, g_ref):
        y = h_t.reshape(8, 64, tm)
        m = jnp.mean(y * y, axis=1, keepdims=True)
        return (y * lax.rsqrt(m + NORM_EPS)) * g_ref[...][None]

    pos = pos_base + i * pos_tile + lax.broadcasted_iota(jnp.int32, (1, tm), 1) * pos_lane
    ang = pos.astype(F32) * inv_ref[...]
    cos = jnp.cos(ang)[None]
    sin = jnp.sin(ang)[None]

    def rope(y):
        x1 = y[:, 0:8, :]
        x2 = y[:, 8:16, :]
        return jnp.concatenate([x1 * cos - x2 * sin, x2 * cos + x1 * sin, y[:, 16:, :]], axis=1)

    fq = head_norm(proj_t(R_FQ, 512), gfq_ref)
    fk = head_norm(proj_t(R_FK, 512), gfk_ref)
    fv = proj_t(R_FV, 512)
    lf = _log_sigmoid(proj_t(R_FF, 16)[0:8, :] + bf_ref[...])
    gzf = _silu(proj_t(R_FZ, 512))
    dq = rope(head_norm(proj_t(R_DQ, 512), gdq_ref))
    dk = rope(head_norm(proj_t(R_DK, 512), gdk_ref))
    dv_t = proj_t(R_DV, 512)
    gzd = _silu(proj_t(R_DZ, 512))
    dv_rm = jnp.dot(xb, wdv_ref[...], preferred_element_type=F32)

    fk2 = fk.reshape(512, tm)
    dk2 = dk.reshape(512, tm)
    fq2 = fq.reshape(512, tm) * QK_SCALE
    dq2 = dq.reshape(512, tm) * QK_SCALE

    if not is_prompt:
        (fk_o, fv_o, lf_o, dk_o, dv_o, fq_o, dq_o, gzf_o, gzd_o) = rest
        fk_o[...] = fk2
        fv_o[...] = fv
        lf_o[...] = lf
        dk_o[...] = dk2
        dv_o[...] = dv_rm
        fq_o[...] = fq2
        dq_o[...] = dq2
        gzf_o[...] = gzf
        gzd_o[...] = gzd
        return

    (fk_o, fv_o, lf_o, dk_o, dv_o, qfa_o, kfa_o, vfa_o, gzf_o, qd_o, kd_o, vda_o, gzd_o, carry) = rest
    fk_o[...] = fk2
    fv_o[...] = fv
    lf_o[...] = lf
    dk_o[...] = dk2
    dv_o[...] = dv_rm
    gzf_o[...] = gzf.astype(BF16)
    gzd_o[...] = gzd.astype(BF16)
    qd_o[...] = dq2.astype(BF16)

    @pl.when(i == 0)
    def _():
        carry[...] = jnp.zeros_like(carry)

    run = carry[:, 0:1]
    chunks = []
    for j in range(tm // 128):
        cj = run + _lane_cumsum(lf[:, j * 128:(j + 1) * 128])
        run = cj[:, 127:128]
        chunks.append(cj)
    carry[...] = jnp.broadcast_to(run, carry.shape)
    c2 = jnp.concatenate(chunks, axis=1) * LOG2E

    hi = c2.astype(BF16).astype(F32)
    r1 = c2 - hi
    mid = r1.astype(BF16).astype(F32)
    lo = (r1 - mid).astype(BF16).astype(F32)

    rowid = lax.broadcasted_iota(jnp.int32, (8, tm), 0)
    ones3 = jnp.where(rowid < 3, 1.0, 0.0)
    zeros48 = jnp.zeros((48, tm), F32)
    one_row16 = jnp.where(lax.broadcasted_iota(jnp.int32, (16, tm), 0) == 0, 1.0, 0.0)

    for h in range(FOX_HEADS):
        cs = jnp.where(rowid == 0, hi[h:h + 1, :],
                       jnp.where(rowid == 1, mid[h:h + 1, :],
                                 jnp.where(rowid == 2, lo[h:h + 1, :], 0.0)))
        q_aug = jnp.concatenate([fq2[h * 64:(h + 1) * 64, :], cs, ones3, zeros48], axis=0)
        k_aug = jnp.concatenate([fk2[h * 64:(h + 1) * 64, :], ones3, -cs, zeros48], axis=0)
        qfa_o[h * FOX_DK:(h + 1) * FOX_DK, :] = q_aug.astype(BF16)
        kfa_o[:, h * FOX_DK:(h + 1) * FOX_DK] = k_aug.T.astype(BF16)
        v_aug = jnp.concatenate([fv[h * 64:(h + 1) * 64, :], one_row16], axis=0)
        vfa_o[h * FOX_DVA:(h + 1) * FOX_DVA, :] = v_aug.astype(BF16)

    for h in range(DIFF_HEADS):
        kd_o[:, h * 128:(h + 1) * 128] = dk2[h * 128:(h + 1) * 128, :].T.astype(BF16)
        v_aug = jnp.concatenate([dv_t[h * 128:(h + 1) * 128, :], one_row16], axis=0)
        vda_o[h * DIFF_DVA:(h + 1) * DIFF_DVA, :] = v_aug.astype(BF16)


def _project(x2d, params, is_prompt, tm, pos_base, pos_tile, pos_lane):
    n_tok = x2d.shape[0]
    n_tiles = n_tok // tm
    full = lambda a: pl.BlockSpec(a.shape, lambda i: (0,) * a.ndim)
    in_specs = [pl.BlockSpec((tm, D_MODEL), lambda i: (i, 0))] + [full(p) for p in params]
    t_spec = lambda rows: pl.BlockSpec((rows, tm), lambda i: (0, i))
    r_spec = lambda cols: pl.BlockSpec((tm, cols), lambda i: (i, 0))
    if is_prompt:
        sds = jax.ShapeDtypeStruct
        out_shape = [
            sds((512, n_tok), F32), sds((512, n_tok), F32), sds((8, n_tok), F32), sds((512, n_tok), F32),
            sds((n_tok, 512), F32),
            sds((FOX_HEADS * FOX_DK, n_tok), BF16), sds((n_tok, FOX_HEADS * FOX_DK), BF16),
            sds((FOX_HEADS * FOX_DVA, n_tok), BF16), sds((512, n_tok), BF16),
            sds((512, n_tok), BF16), sds((n_tok, 512), BF16),
            sds((DIFF_HEADS * DIFF_DVA, n_tok), BF16), sds((512, n_tok), BF16),
        ]
        out_specs = [
            t_spec(512), t_spec(512), t_spec(8), t_spec(512), r_spec(512),
            t_spec(FOX_HEADS * FOX_DK), r_spec(FOX_HEADS * FOX_DK),
            t_spec(FOX_HEADS * FOX_DVA), t_spec(512),
            t_spec(512), r_spec(512), t_spec(DIFF_HEADS * DIFF_DVA), t_spec(512),
        ]
        scratch = [pltpu.VMEM((8, 128), F32)]
    else:
        sds = jax.ShapeDtypeStruct
        s3 = lambda rows: pl.BlockSpec((None, rows, tm), lambda i: (i, 0, 0))
        out_shape = [
            sds((n_tiles, 512, tm), F32), sds((n_tiles, 512, tm), F32), sds((n_tiles, 8, tm), F32),
            sds((n_tiles, 512, tm), F32), sds((n_tiles, tm, 512), F32),
            sds((n_tiles, 512, tm), F32), sds((n_tiles, 512, tm), F32),
            sds((n_tiles, 512, tm), F32), sds((n_tiles, 512, tm), F32),
        ]
        out_specs = [s3(512), s3(512), s3(8), s3(512),
                     pl.BlockSpec((None, tm, 512), lambda i: (i, 0, 0)),
                     s3(512), s3(512), s3(512), s3(512)]
        scratch = []
    return pl.pallas_call(
        functools.partial(_proj_body, is_prompt, tm, pos_base, pos_tile, pos_lane),
        grid=(n_tiles,),
        in_specs=in_specs,
        out_specs=out_specs,
        out_shape=out_shape,
        scratch_shapes=scratch,
        compiler_params=pltpu.CompilerParams(dimension_semantics=("arbitrary",),
                                             vmem_limit_bytes=VMEM_LIMIT),
        name="proj_prompt" if is_prompt else "proj_sample",
    )(x2d, *params)


def _attn_body(is_diff, tq, tk, dk, dva, *refs):
    qi_tab, ki_tab, qT_ref, k_ref, v_ref, gz_ref = refs[:6]
    if is_diff:
        lq1, lk1, lq2, lk2, gs_ref, og_ref, m_s, acc_s = refs[6:]
    else:
        og_ref, m_s, acc_s = refs[6:]
    p = pl.program_id(1)
    qi = qi_tab[p]
    ki = ki_tab[p]

    @pl.when(ki == 0)
    def _():
        m_s[...] = jnp.full(m_s.shape, NEG, F32)
        acc_s[...] = jnp.zeros(acc_s.shape, F32)

    def update(masked):
        for j in range(2):
            s = jnp.dot(k_ref[:, j * dk:(j + 1) * dk], qT_ref[j * dk:(j + 1) * dk, :],
                        preferred_element_type=F32)
            if masked:
                kpos = ki * tk + lax.broadcasted_iota(jnp.int32, (tk, tq), 0)
                qpos = qi * tq + lax.broadcasted_iota(jnp.int32, (tk, tq), 1)
                s = jnp.where(kpos <= qpos, s, NEG)
            m_old = m_s[j]
            m_new = jnp.maximum(m_old, jnp.max(s, axis=0, keepdims=True))
            alpha = jnp.exp2(m_old - m_new)
            pb = jnp.exp2(s - m_new).astype(BF16)
            vj = v_ref[...] if is_diff else v_ref[j * dva:(j + 1) * dva, :]
            acc_s[j] = alpha * acc_s[j] + jnp.dot(vj, pb, preferred_element_type=F32)
            m_s[j] = m_new

    need_mask = (ki + 1) * tk - 1 > qi * tq

    @pl.when(need_mask)
    def _():
        update(True)

    @pl.when(jnp.logical_not(need_mask))
    def _():
        update(False)

    @pl.when(ki == ((qi + 1) * tq - 1) // tk)
    def _():
        if is_diff:
            lam = _diff_lambda(lq1, lk1, lq2, lk2)
            a1 = acc_s[0]
            a2 = acc_s[1]
            od = a1[0:128, :] / a1[128:129, :] - lam * (a2[0:128, :] / a2[128:129, :])
            ms = jnp.mean(od * od, axis=0, keepdims=True)
            y = (od * lax.rsqrt(ms + NORM_EPS)) * gs_ref[...] * (1.0 - LAMBDA_INIT)
            og_ref[...] = (y * gz_ref[...].astype(F32)).astype(BF16)
        else:
            for j in range(2):
                a = acc_s[j]
                o = a[0:64, :] / a[64:65, :]
                og_ref[j * 64:(j + 1) * 64, :] = (o * gz_ref[j * 64:(j + 1) * 64, :].astype(F32)).astype(BF16)


def _causal_pairs(n_tok, tq, tk):
    qi_l, ki_l = [], []
    for qi in range(n_tok // tq):
        for ki in range(((qi + 1) * tq - 1) // tk + 1):
            qi_l.append(qi)
            ki_l.append(ki)
    return jnp.asarray(qi_l, jnp.int32), jnp.asarray(ki_l, jnp.int32)


def _prompt_attention(is_diff, qT, k, vT, gzT, extra, n_tok, tq, tk):
    dk = DIFF_QK_DIM if is_diff else FOX_DK
    dva = DIFF_DVA if is_diff else FOX_DVA
    v_rows = dva if is_diff else 2 * dva
    qi_tab, ki_tab = _causal_pairs(n_tok, tq, tk)
    n_pairs = qi_tab.shape[0]
    in_specs = [
        pl.BlockSpec((2 * dk, tq), lambda h, p, qt, kt: (h, qt[p])),
        pl.BlockSpec((tk, 2 * dk), lambda h, p, qt, kt: (kt[p], h)),
        pl.BlockSpec((v_rows, tk), lambda h, p, qt, kt: (h, kt[p])),
        pl.BlockSpec((128, tq), lambda h, p, qt, kt: (h, qt[p])),
    ] + [pl.BlockSpec(e.shape, lambda h, p, qt, kt: (0, 0)) for e in extra]
    grid_spec = pltpu.PrefetchScalarGridSpec(
        num_scalar_prefetch=2,
        grid=(4, n_pairs),
        in_specs=in_specs,
        out_specs=pl.BlockSpec((128, tq), lambda h, p, qt, kt: (h, qt[p])),
        scratch_shapes=[pltpu.VMEM((2, 1, tq), F32), pltpu.VMEM((2, dva, tq), F32)],
    )
    return pl.pallas_call(
        functools.partial(_attn_body, is_diff, tq, tk, dk, dva),
        grid_spec=grid_spec,
        out_shape=jax.ShapeDtypeStruct((512, n_tok), BF16),
        compiler_params=pltpu.CompilerParams(dimension_semantics=("arbitrary", "arbitrary"),
                                             vmem_limit_bytes=VMEM_LIMIT),
        name="attn_diff" if is_diff else "attn_fox",
    )(qi_tab, ki_tab, qT, k, vT, gzT, *extra)


def _sample_body(pg, ng, *refs):
    idx = 1
    fk_p = refs[idx:idx + pg]; idx += pg
    fv_p = refs[idx:idx + pg]; idx += pg
    lf_p = refs[idx:idx + pg]; idx += pg
    dk_p = refs[idx:idx + pg]; idx += pg
    dv_p = refs[idx:idx + pg]; idx += pg
    (qf_ref, qd_ref, kfn_ref, vfn_ref, lfn_ref, kdn_ref, vdn_ref, gzf_ref, gzd_ref,
     lq1, lk1, lq2, lk2, gs_ref) = refs[idx:idx + 14]
    idx += 14
    og_ref = refs[idx]
    (qbf_s, qbd_s, mf_s, lf_s, accf_s, md_s, ld_s, accd_s, off_s,
     nk_s, nv_s, nl_s, ndk_s, ndv_s) = refs[idx + 1:]
    g = pl.program_id(1)

    row = lax.broadcasted_iota(jnp.int32, (64, 512), 0)
    col = lax.broadcasted_iota(jnp.int32, (64, 512), 1)
    blockdiag = (row // 8) == (col // 64)

    @pl.when(g == 0)
    def _():
        qf = jnp.concatenate([qf_ref[0]] * 8, axis=0)
        qd = jnp.concatenate([qd_ref[0]] * 8, axis=0)
        qbf_s[...] = jnp.where(blockdiag, qf, 0.0).astype(BF16)
        qbd_s[...] = jnp.where(blockdiag, qd, 0.0).astype(BF16)
        mf_s[...] = jnp.full(mf_s.shape, NEG, F32)
        md_s[...] = jnp.full(md_s.shape, NEG, F32)
        lf_s[...] = jnp.zeros(lf_s.shape, F32)
        ld_s[...] = jnp.zeros(ld_s.shape, F32)
        accf_s[...] = jnp.zeros(accf_s.shape, F32)
        accd_s[...] = jnp.zeros(accd_s.shape, F32)
        off_s[...] = jnp.zeros(off_s.shape, F32)

    def attend(pages, mask):
        n = len(pages)
        off = off_s[:, 0:1]
        sf, sd = [], []
        for (kT, _, lfp, dkT, _) in pages:
            cw = _lane_cumsum(lfp)
            c = (off + cw) * LOG2E
            off = off + cw[:, 127:128]
            s = jnp.dot(qbf_s[...], kT.astype(BF16), preferred_element_type=F32)
            s = s - jnp.broadcast_to(c[:, None, :], (8, 8, 128)).reshape(64, 128)
            s2 = jnp.dot(qbd_s[...], dkT.astype(BF16), preferred_element_type=F32)
            if mask is not None:
                s = jnp.where(mask, s, NEG)
                s2 = jnp.where(mask, s2, NEG)
            sf.append(s)
            sd.append(s2)
        off_s[...] = jnp.broadcast_to(off, off_s.shape)
        s_f = sf[0] if n == 1 else jnp.concatenate(sf, axis=1)
        s_d = sd[0] if n == 1 else jnp.concatenate(sd, axis=1)

        m_old = mf_s[...]
        m_new = jnp.maximum(m_old, jnp.max(s_f, axis=1, keepdims=True))
        alpha = jnp.exp2(m_old - m_new)
        pr = jnp.exp2(s_f - m_new)
        lf_s[...] = alpha * lf_s[...] + jnp.sum(pr, axis=1, keepdims=True)
        mf_s[...] = m_new
        pb = pr.astype(BF16)
        pv = None
        for i, (_, vT, _, _, _) in enumerate(pages):
            t = lax.dot_general(pb[:, i * 128:(i + 1) * 128], vT.astype(BF16),
                                (((1,), (1,)), ((), ())), preferred_element_type=F32)
            pv = t if pv is None else pv + t
        accf_s[...] = alpha * accf_s[...] + pv

        m_old = md_s[...]
        m_new = jnp.maximum(m_old, jnp.max(s_d, axis=1, keepdims=True))
        alpha = jnp.exp2(m_old - m_new)
        pr = jnp.exp2(s_d - m_new)
        ld_s[...] = alpha * ld_s[...] + jnp.sum(pr, axis=1, keepdims=True)
        md_s[...] = m_new
        pb = pr.astype(BF16)
        for h in range(DIFF_HEADS):
            pv = None
            for i, (_, _, _, _, dv_ref) in enumerate(pages):
                vh = dv_ref[pl.ds(h, 128, stride=4), :].astype(BF16)
                t = jnp.dot(pb[h * 16:(h + 1) * 16, i * 128:(i + 1) * 128], vh,
                            preferred_element_type=F32)
                pv = t if pv is None else pv + t
            accd_s[h * 16:(h + 1) * 16, :] = alpha[h * 16:(h + 1) * 16, :] * accd_s[h * 16:(h + 1) * 16, :] + pv

    attend([(fk_p[i][0], fv_p[i][0], lf_p[i][0], dk_p[i][0], dv_p[i].at[0]) for i in range(pg)], None)

    @pl.when(g == ng - 1)
    def _():
        nk_s[...] = jnp.zeros(nk_s.shape, F32)
        nv_s[...] = jnp.zeros(nv_s.shape, F32)
        nl_s[...] = jnp.zeros(nl_s.shape, F32)
        ndk_s[...] = jnp.zeros(ndk_s.shape, F32)
        ndv_s[...] = jnp.zeros(ndv_s.shape, F32)
        nk_s[:, 0:8] = kfn_ref[0]
        nv_s[:, 0:8] = vfn_ref[0]
        nl_s[:, 0:8] = lfn_ref[0]
        ndk_s[:, 0:8] = kdn_ref[0]
        ndv_s[0:32, :] = vdn_ref[0]
        r = lax.broadcasted_iota(jnp.int32, (64, 128), 0)
        c = lax.broadcasted_iota(jnp.int32, (64, 128), 1)
        attend([(nk_s[...], nv_s[...], nl_s[...], ndk_s[...], ndv_s)], c <= (r & 7))

        of = jnp.where(blockdiag, accf_s[...] / lf_s[...], 0.0)
        of = jnp.sum(of.reshape(8, 8, 512), axis=0)
        ogf = of * gzf_ref[0]
        lam = _diff_lambda(lq1, lk1, lq2, lk2)
        on = accd_s[...] / ld_s[...]
        outs = [ogf]
        for h in range(DIFF_HEADS):
            od = on[h * 16:h * 16 + 8, :] - lam * on[h * 16 + 8:h * 16 + 16, :]
            ms = jnp.mean(od * od, axis=-1, keepdims=True)
            y = (od * lax.rsqrt(ms + NORM_EPS)) * gs_ref[...] * (1.0 - LAMBDA_INIT)
            outs.append(y * gzd_ref[0][:, h * 128:(h + 1) * 128])
        og_ref[0] = jnp.concatenate(outs, axis=1).astype(BF16)


def _sample_attention(page_table, pools, per_seq, extra, pg):
    n_seq, n_pages = page_table.shape
    ng = n_pages // pg
    pt_flat = page_table.reshape(-1)
    fk_pool, fv_pool, lf_pool, dk_pool, dv_pool = pools

    def page_spec(shape, i):
        return pl.BlockSpec((1,) + shape, lambda b, g, pt: (pt[b * n_pages + g * pg + i], 0, 0))

    in_specs = []
    operands = []
    for pool, shape in ((fk_pool, (512, 128)), (fv_pool, (512, 128)), (lf_pool, (8, 128)),
                        (dk_pool, (512, 128)), (dv_pool, (512, 128))):
        for i in range(pg):
            in_specs.append(page_spec(shape, i))
            operands.append(pool)
    for a in per_seq:
        in_specs.append(pl.BlockSpec((1,) + a.shape[1:], lambda b, g, pt: (b, 0, 0)))
        operands.append(a)
    for e in extra:
        in_specs.append(pl.BlockSpec(e.shape, lambda b, g, pt: (0, 0)))
        operands.append(e)
    scratch = [
        pltpu.VMEM((64, 512), BF16), pltpu.VMEM((64, 512), BF16),
        pltpu.VMEM((64, 1), F32), pltpu.VMEM((64, 1), F32), pltpu.VMEM((64, 512), F32),
        pltpu.VMEM((64, 1), F32), pltpu.VMEM((64, 1), F32), pltpu.VMEM((64, 128), F32),
        pltpu.VMEM((8, 128), F32),
        pltpu.VMEM((512, 128), F32), pltpu.VMEM((512, 128), F32), pltpu.VMEM((8, 128), F32),
        pltpu.VMEM((512, 128), F32), pltpu.VMEM((512, 128), F32),
    ]
    grid_spec = pltpu.PrefetchScalarGridSpec(
        num_scalar_prefetch=1,
        grid=(n_seq, ng),
        in_specs=in_specs,
        out_specs=pl.BlockSpec((1, 8, 1024), lambda b, g, pt: (b, 0, 0)),
        scratch_shapes=scratch,
    )
    return pl.pallas_call(
        functools.partial(_sample_body, pg, ng),
        grid_spec=grid_spec,
        out_shape=jax.ShapeDtypeStruct((n_seq, 8, 1024), BF16),
        compiler_params=pltpu.CompilerParams(dimension_semantics=("arbitrary", "arbitrary"),
                                             vmem_limit_bytes=VMEM_LIMIT),
        name="attn_sample",
    )(pt_flat, *operands)


def _out_t_body(x_ref, of_ref, od_ref, woT_ref, o_ref):
    y_t = jnp.dot(woT_ref[:, 0:512], of_ref[...], preferred_element_type=F32)
    y_t = y_t + jnp.dot(woT_ref[:, 512:1024], od_ref[...], preferred_element_type=F32)
    o_ref[...] = x_ref[...] + y_t.T


def _out_proj_t(x2d, ogf_t, ogd_t, wo_t, tm):
    n_tok = x2d.shape[0]
    return pl.pallas_call(
        _out_t_body,
        grid=(n_tok // tm,),
        in_specs=[pl.BlockSpec((tm, D_MODEL), lambda i: (i, 0)),
                  pl.BlockSpec((512, tm), lambda i: (0, i)),
                  pl.BlockSpec((512, tm), lambda i: (0, i)),
                  pl.BlockSpec((D_MODEL, D_MODEL), lambda i: (0, 0))],
        out_specs=pl.BlockSpec((tm, D_MODEL), lambda i: (i, 0)),
        out_shape=jax.ShapeDtypeStruct((n_tok, D_MODEL), F32),
        compiler_params=pltpu.CompilerParams(dimension_semantics=("arbitrary",),
                                             vmem_limit_bytes=VMEM_LIMIT),
        name="out_proj_prompt",
    )(x2d, ogf_t, ogd_t, wo_t)


def _out_body(x_ref, og_ref, wo_ref, o_ref):
    o_ref[...] = x_ref[...] + jnp.dot(og_ref[...], wo_ref[...], preferred_element_type=F32)


def _out_proj(x2d, og, wo, tm):
    n_tok = x2d.shape[0]
    return pl.pallas_call(
        _out_body,
        grid=(n_tok // tm,),
        in_specs=[pl.BlockSpec((tm, D_MODEL), lambda i: (i, 0)),
                  pl.BlockSpec((tm, D_MODEL), lambda i: (i, 0)),
                  pl.BlockSpec((D_MODEL, D_MODEL), lambda i: (0, 0))],
        out_specs=pl.BlockSpec((tm, D_MODEL), lambda i: (i, 0)),
        out_shape=jax.ShapeDtypeStruct((n_tok, D_MODEL), F32),
        compiler_params=pltpu.CompilerParams(dimension_semantics=("arbitrary",),
                                             vmem_limit_bytes=VMEM_LIMIT),
        name="out_proj_sample",
    )(x2d, og, wo)


def kernel(x_prompt, x_sample, cache_fox_k, cache_fox_v, cache_fox_logf, cache_diff_k, cache_diff_v,
           page_table, g_norm, w_in, b_f, g_fox_q, g_fox_k, g_diff_q, g_diff_k,
           lam_q1, lam_k1, lam_q2, lam_k2, g_subln, w_out):
    seq = x_prompt.shape[1]
    n_seq, dec = x_sample.shape[0], x_sample.shape[1]
    past = page_table.shape[1] * PAGE_SIZE
    n_pool = cache_fox_k.shape[1]

    w = w_in[0]
    sizes = [FOX_WIDTH, FOX_WIDTH, FOX_WIDTH, FOX_HEADS, FOX_WIDTH, DIFF_QK_WIDTH, DIFF_QK_WIDTH,
             DIFF_WIDTH, DIFF_WIDTH]
    offs = [0]
    for s in sizes:
        offs.append(offs[-1] + s)
    col = lambda k: w[:, offs[k]:offs[k + 1]]
    w_t = jnp.concatenate(
        [col(0), col(1), col(2), col(4), col(5), col(6), col(7), col(8), col(3),
         jnp.zeros((D_MODEL, 8), w.dtype)], axis=1).T.astype(BF16)
    w_dv = col(7).astype(BF16)
    half = ROT_DIM // 2
    inv = (ROPE_THETA ** (-jnp.arange(half, dtype=F32) * 2.0 / ROT_DIM)).reshape(half, 1)
    colv = lambda a: a[0].astype(F32).reshape(-1, 1)
    params = (g_norm.astype(F32), w_t, w_dv, colv(b_f), colv(g_fox_q), colv(g_fox_k),
              colv(g_diff_q), colv(g_diff_k), inv)
    lam_params = (lam_q1.astype(F32), lam_k1.astype(F32), lam_q2.astype(F32), lam_k2.astype(F32))
    wo = w_out[0].astype(BF16)

    xp2 = x_prompt[0]
    (fk_t, fv_t, lf_t, dk_t, dv_p, qfa_t, kfa, vfa_t, gzf_t, qd_t, kd, vda_t, gzd_t) = _project(
        xp2, params, True, 256, 0, 256, 1)
    tq = tk = 1024
    ogf_t = _prompt_attention(False, qfa_t, kfa, vfa_t, gzf_t, (), seq, tq, tk)
    ogd_t = _prompt_attention(True, qd_t, kd, vda_t, gzd_t,
                              lam_params + (colv(g_subln),), seq, tq, tk)
    y_prompt = _out_proj_t(xp2, ogf_t, ogd_t, wo.T, 256)[None]

    xs_tb = jnp.transpose(x_sample, (1, 0, 2)).reshape(dec * n_seq, D_MODEL)
    (sfk, sfv, slf, sdk, sdv, sfq, sdq, sgzf, sgzd) = _project(
        xs_tb, params, False, n_seq, past, 1, 0)
    bt = lambda a: jnp.transpose(a, (2, 0, 1))
    bf = lambda a: jnp.transpose(a, (2, 1, 0))
    per_seq = (bt(sfq), bt(sdq), bf(sfk), bf(sfv), bf(slf), bf(sdk),
               jnp.transpose(sdv, (1, 0, 2)).reshape(n_seq, dec * DIFF_HEADS, DIFF_V_DIM),
               bt(sgzf), bt(sgzd))
    pools = (
        jnp.transpose(cache_fox_k[0], (0, 2, 3, 1)).reshape(n_pool, 512, PAGE_SIZE),
        jnp.transpose(cache_fox_v[0], (0, 2, 3, 1)).reshape(n_pool, 512, PAGE_SIZE),
        jnp.transpose(cache_fox_logf[0], (0, 2, 1)),
        jnp.transpose(cache_diff_k[0], (0, 2, 3, 4, 1)).reshape(n_pool, 512, PAGE_SIZE),
        cache_diff_v[0].reshape(n_pool, PAGE_SIZE * DIFF_HEADS, DIFF_V_DIM),
    )
    og_s = _sample_attention(page_table, pools, per_seq,
                             lam_params + (g_subln.astype(F32),), 8)
    y_sample = _out_proj(x_sample.reshape(n_seq * dec, D_MODEL), og_s.reshape(n_seq * dec, D_MODEL),
                         wo, 256).reshape(n_seq, dec, D_MODEL)

    new_fk_p = jnp.transpose(fk_t.reshape(FOX_HEADS, HEAD_DIM, seq), (2, 0, 1))[None, None]
    new_fv_p = jnp.transpose(fv_t.reshape(FOX_HEADS, HEAD_DIM, seq), (2, 0, 1))[None, None]
    new_fl_p = jnp.transpose(lf_t, (1, 0))[None, None]
    new_dk_p = jnp.transpose(dk_t.reshape(DIFF_HEADS, 2, DIFF_QK_DIM, seq), (3, 0, 1, 2))[None, None]
    new_dv_p = dv_p.reshape(seq, DIFF_HEADS, DIFF_V_DIM)[None, None]
    new_fk_s = jnp.transpose(sfk.reshape(dec, FOX_HEADS, HEAD_DIM, n_seq), (3, 0, 1, 2))[None]
    new_fv_s = jnp.transpose(sfv.reshape(dec, FOX_HEADS, HEAD_DIM, n_seq), (3, 0, 1, 2))[None]
    new_fl_s = jnp.transpose(slf, (2, 0, 1))[None]
    new_dk_s = jnp.transpose(sdk.reshape(dec, DIFF_HEADS, 2, DIFF_QK_DIM, n_seq), (4, 0, 1, 2, 3))[None]
    new_dv_s = jnp.transpose(sdv, (1, 0, 2)).reshape(n_seq, dec, DIFF_HEADS, DIFF_V_DIM)[None]
    return (y_prompt, y_sample, new_fk_p, new_fv_p, new_fl_p, new_dk_p, new_dv_p,
            new_fk_s, new_fv_s, new_fl_s, new_dk_s, new_dv_s)
```

```python
import functools

import jax
import jax.numpy as jnp
from jax import lax
from jax.experimental import pallas as pl
from jax.experimental.pallas import tpu as pltpu

D_MODEL = 1024
HEAD_DIM = 64
FOX_HEADS = 8
FOX_WIDTH = FOX_HEADS * HEAD_DIM
DIFF_HEADS = 4
DIFF_QK_DIM = 64
DIFF_V_DIM = 2 * DIFF_QK_DIM
DIFF_QK_WIDTH = DIFF_HEADS * 2 * DIFF_QK_DIM
DIFF_WIDTH = DIFF_HEADS * DIFF_V_DIM
PAGE_SIZE = 128
ROPE_THETA = 500000.0
ROT_DIM = DIFF_QK_DIM // 4
NORM_EPS = 1e-6
LAMBDA_INIT = 0.2

LOG2E = 1.4426950408889634
QK_SCALE = LOG2E * HEAD_DIM ** -0.5
NEG = -1e30

F32 = jnp.float32
BF16 = jnp.bfloat16

AUG_DK = 128
SHIFT_LIMIT = 60.0
FOX_DVA = 80
DIFF_DVA = 144

VMEM_LIMIT = 48 * 1024 * 1024

R_FQ, R_FK, R_FV, R_FZ, R_DQ, R_DK, R_DV, R_DZ, R_FF = (i * 512 for i in range(9))
WT_ROWS = R_FF + 16


def _lane_cumsum(x):
    lane = lax.broadcasted_iota(jnp.int32, x.shape, 1)
    sh = 1
    while sh < x.shape[1]:
        x = x + jnp.where(lane >= sh, pltpu.roll(x, sh, 1), 0.0)
        sh *= 2
    return x


def _silu(z):
    return z * (1.0 / (1.0 + jnp.exp(-z)))


def _log_sigmoid(z):
    return jnp.minimum(z, 0.0) - jnp.log1p(jnp.exp(-jnp.abs(z)))


def _diff_lambda(lq1, lk1, lq2, lk2):
    a = jnp.sum(lq1[...] * lk1[...], axis=-1, keepdims=True)
    b = jnp.sum(lq2[...] * lk2[...], axis=-1, keepdims=True)
    return jnp.exp(a) - jnp.exp(b) + LAMBDA_INIT


def _proj_body(is_prompt, tm, pos_base, pos_tile, pos_lane, *refs):
    (x_ref, gn_ref, wT_ref, wdv_ref, bf_ref, gfq_ref, gfk_ref, gdq_ref, gdk_ref, inv_ref) = refs[:10]
    rest = refs[10:]
    i = pl.program_id(0)

    x = x_ref[...]
    ms = jnp.mean(x * x, axis=-1, keepdims=True)
    xn = (x * lax.rsqrt(ms + NORM_EPS)) * gn_ref[...]
    xb = xn.astype(BF16)

    def proj_t(r0, n):
        return lax.dot_general(wT_ref[r0:r0 + n, :], xb, (((1,), (1,)), ((), ())),
                               preferred_element_type=F32)

    def head_norm(h_t, g_ref):
        y = h_t.reshape(8, 64, tm)
        m = jnp.mean(y * y, axis=1, keepdims=True)
        return (y * lax.rsqrt(m + NORM_EPS)) * g_ref[...][None]

    pos = pos_base + i * pos_tile + lax.broadcasted_iota(jnp.int32, (1, tm), 1) * pos_lane
    ang = pos.astype(F32) * inv_ref[...]
    cos = jnp.cos(ang)[None]
    sin = jnp.sin(ang)[None]

    def rope(y):
        x1 = y[:, 0:8, :]
        x2 = y[:, 8:16, :]
        return jnp.concatenate([x1 * cos - x2 * sin, x2 * cos + x1 * sin, y[:, 16:, :]], axis=1)

    fq = head_norm(proj_t(R_FQ, 512), gfq_ref)
    fk = head_norm(proj_t(R_FK, 512), gfk_ref)
    fv = proj_t(R_FV, 512)
    lf = _log_sigmoid(proj_t(R_FF, 16)[0:8, :] + bf_ref[...])
    gzf = _silu(proj_t(R_FZ, 512))
    dq = rope(head_norm(proj_t(R_DQ, 512), gdq_ref))
    dk = rope(head_norm(proj_t(R_DK, 512), gdk_ref))
    dv_t = proj_t(R_DV, 512)
    gzd = _silu(proj_t(R_DZ, 512))
    dv_rm = jnp.dot(xb, wdv_ref[...], preferred_element_type=F32)

    fk2 = fk.reshape(512, tm)
    dk2 = dk.reshape(512, tm)
    fq2 = fq.reshape(512, tm) * QK_SCALE
    dq2 = dq.reshape(512, tm) * QK_SCALE

    if not is_prompt:
        (fk_o, fv_o, lf_o, dk_o, dv_o, fq_o, dq_o, gzf_o, gzd_o) = rest
        fk_o[...] = fk2
        fv_o[...] = fv
        lf_o[...] = lf
        dk_o[...] = dk2
        dv_o[...] = dv_rm
        fq_o[...] = fq2
        dq_o[...] = dq2
        gzf_o[...] = gzf
        gzd_o[...] = gzd
        return

    (fk_o, fv_o, lf_o, dk_o, dv_o, qfa_o, kfa_o, vfa_o, gzf_o, qd_o, kd_o, vda_o, gzd_o, carry) = rest
    fk_o[...] = fk2
    fv_o[...] = fv
    lf_o[...] = lf
    dk_o[...] = dk2
    for h in range(DIFF_HEADS):
        dv_o[pl.ds(h, tm, stride=DIFF_HEADS), :] = dv_rm[:, h * DIFF_V_DIM:(h + 1) * DIFF_V_DIM]
    gzf_o[...] = gzf.astype(BF16)
    gzd_o[...] = gzd.astype(BF16)

    @pl.when(i == 0)
    def _():
        carry[...] = jnp.zeros_like(carry)

    run = carry[:, 0:1]
    chunks = []
    for j in range(tm // 128):
        cj = run + _lane_cumsum(lf[:, j * 128:(j + 1) * 128])
        run = cj[:, 127:128]
        chunks.append(cj)
    carry[...] = jnp.broadcast_to(run, carry.shape)
    c2 = jnp.concatenate(chunks, axis=1) * LOG2E

    def split3(v):
        hi = v.astype(BF16).astype(F32)
        r1 = v - hi
        mid = r1.astype(BF16).astype(F32)
        return hi, mid, (r1 - mid).astype(BF16).astype(F32)

    rowid = lax.broadcasted_iota(jnp.int32, (8, tm), 0)

    def slab(pieces, h):
        return jnp.where(rowid == 0, pieces[0][h:h + 1, :],
                         jnp.where(rowid == 1, pieces[1][h:h + 1, :],
                                   jnp.where(rowid == 2, pieces[2][h:h + 1, :], 0.0)))

    def shift_bound(q2, g_ref):
        n = jnp.sqrt(jnp.sum(jnp.square(q2.reshape(8, 64, tm)), axis=1))
        return n * (8.0 * jnp.max(jnp.abs(g_ref[...]), axis=0, keepdims=True))

    c_pieces = split3(c2)
    mf_pieces = split3(shift_bound(fq2, gfk_ref))
    md_pieces = split3(shift_bound(dq2, gdk_ref))
    ones3 = jnp.where(rowid < 3, 1.0, 0.0)
    zeros8 = jnp.zeros((8, tm), F32)
    zeros40 = jnp.zeros((40, tm), F32)
    one_row16 = jnp.where(lax.broadcasted_iota(jnp.int32, (16, tm), 0) == 0, 1.0, 0.0)

    for h in range(FOX_HEADS):
        cs = slab(c_pieces, h)
        q_aug = jnp.concatenate([fq2[h * 64:(h + 1) * 64, :], cs, ones3, -slab(mf_pieces, h), zeros40], axis=0)
        k_aug = jnp.concatenate([fk2[h * 64:(h + 1) * 64, :], ones3, -cs, ones3, zeros40], axis=0)
        qfa_o[h * AUG_DK:(h + 1) * AUG_DK, :] = q_aug.astype(BF16)
        kfa_o[:, h * AUG_DK:(h + 1) * AUG_DK] = k_aug.T.astype(BF16)
        v_aug = jnp.concatenate([fv[h * 64:(h + 1) * 64, :], one_row16], axis=0)
        vfa_o[h * FOX_DVA:(h + 1) * FOX_DVA, :] = v_aug.astype(BF16)

    for j in range(2 * DIFF_HEADS):
        q_aug = jnp.concatenate([dq2[j * 64:(j + 1) * 64, :], zeros8, zeros8, -slab(md_pieces, j), zeros40], axis=0)
        k_aug = jnp.concatenate([dk2[j * 64:(j + 1) * 64, :], zeros8, zeros8, ones3, zeros40], axis=0)
        qd_o[j * AUG_DK:(j + 1) * AUG_DK, :] = q_aug.astype(BF16)
        kd_o[:, j * AUG_DK:(j + 1) * AUG_DK] = k_aug.T.astype(BF16)

    for h in range(DIFF_HEADS):
        v_aug = jnp.concatenate([dv_t[h * 128:(h + 1) * 128, :], one_row16], axis=0)
        vda_o[h * DIFF_DVA:(h + 1) * DIFF_DVA, :] = v_aug.astype(BF16)


def _project(x2d, params, is_prompt, tm, pos_base, pos_tile, pos_lane):
    n_tok = x2d.shape[0]
    n_tiles = n_tok // tm
    full = lambda a: pl.BlockSpec(a.shape, lambda i: (0,) * a.ndim)
    in_specs = [pl.BlockSpec((tm, D_MODEL), lambda i: (i, 0))] + [full(p) for p in params]
    t_spec = lambda rows: pl.BlockSpec((rows, tm), lambda i: (0, i))
    r_spec = lambda cols: pl.BlockSpec((tm, cols), lambda i: (i, 0))
    if is_prompt:
        sds = jax.ShapeDtypeStruct
        out_shape = [
            sds((512, n_tok), F32), sds((512, n_tok), F32), sds((8, n_tok), F32), sds((512, n_tok), F32),
            sds((DIFF_HEADS * n_tok, DIFF_V_DIM), F32),
            sds((8 * AUG_DK, n_tok), BF16), sds((n_tok, 8 * AUG_DK), BF16),
            sds((FOX_HEADS * FOX_DVA, n_tok), BF16), sds((512, n_tok), BF16),
            sds((8 * AUG_DK, n_tok), BF16), sds((n_tok, 8 * AUG_DK), BF16),
            sds((DIFF_HEADS * DIFF_DVA, n_tok), BF16), sds((512, n_tok), BF16),
        ]
        out_specs = [
            t_spec(512), t_spec(512), t_spec(8), t_spec(512),
            pl.BlockSpec((DIFF_HEADS * tm, DIFF_V_DIM), lambda i: (i, 0)),
            t_spec(8 * AUG_DK), r_spec(8 * AUG_DK),
            t_spec(FOX_HEADS * FOX_DVA), t_spec(512),
            t_spec(8 * AUG_DK), r_spec(8 * AUG_DK), t_spec(DIFF_HEADS * DIFF_DVA), t_spec(512),
        ]
        scratch = [pltpu.VMEM((8, 128), F32)]
    else:
        sds = jax.ShapeDtypeStruct
        s3 = lambda rows: pl.BlockSpec((None, rows, tm), lambda i: (i, 0, 0))
        out_shape = [
            sds((n_tiles, 512, tm), F32), sds((n_tiles, 512, tm), F32), sds((n_tiles, 8, tm), F32),
            sds((n_tiles, 512, tm), F32), sds((n_tiles, tm, 512), F32),
            sds((n_tiles, 512, tm), F32), sds((n_tiles, 512, tm), F32),
            sds((n_tiles, 512, tm), F32), sds((n_tiles, 512, tm), F32),
        ]
        out_specs = [s3(512), s3(512), s3(8), s3(512),
                     pl.BlockSpec((None, tm, 512), lambda i: (i, 0, 0)),
                     s3(512), s3(512), s3(512), s3(512)]
        scratch = []
    return pl.pallas_call(
        functools.partial(_proj_body, is_prompt, tm, pos_base, pos_tile, pos_lane),
        grid=(n_tiles,),
        in_specs=in_specs,
        out_specs=out_specs,
        out_shape=out_shape,
        scratch_shapes=scratch,
        compiler_params=pltpu.CompilerParams(dimension_semantics=("arbitrary",),
                                             vmem_limit_bytes=VMEM_LIMIT),
        name="proj_prompt" if is_prompt else "proj_sample",
    )(x2d, *params)


def _attn_body(is_diff, tq, tk, dva, *refs):
    qi_tab, ki_tab, fixed_tab, qT_ref, k_ref, v_ref, gz_ref = refs[:7]
    if is_diff:
        lq1, lk1, lq2, lk2, gs_ref, og_ref, m_s, acc_s = refs[7:]
    else:
        og_ref, m_s, acc_s = refs[7:]
    dk = AUG_DK
    p = pl.program_id(1)
    qi = qi_tab[p]
    ki = ki_tab[p]
    fixed_shift = fixed_tab[0] == 1

    @pl.when(ki == 0)
    def _():
        m_s[...] = jnp.full(m_s.shape, NEG, F32)
        acc_s[...] = jnp.zeros(acc_s.shape, F32)

    def update(masked, fixed):
        for j in range(2):
            s = jnp.dot(k_ref[:, j * dk:(j + 1) * dk], qT_ref[j * dk:(j + 1) * dk, :],
                        preferred_element_type=F32)
            if masked:
                kpos = ki * tk + lax.broadcasted_iota(jnp.int32, (tk, tq), 0)
                qpos = qi * tq + lax.broadcasted_iota(jnp.int32, (tk, tq), 1)
                s = jnp.where(kpos <= qpos, s, NEG)
            vj = v_ref[...] if is_diff else v_ref[j * dva:(j + 1) * dva, :]
            if fixed:
                acc_s[j] += jnp.dot(vj, jnp.exp2(s).astype(BF16), preferred_element_type=F32)
            else:
                m_old = m_s[j]
                m_new = jnp.maximum(m_old, jnp.max(s, axis=0, keepdims=True))
                alpha = jnp.exp2(m_old - m_new)
                pb = jnp.exp2(s - m_new).astype(BF16)
                acc_s[j] = alpha * acc_s[j] + jnp.dot(vj, pb, preferred_element_type=F32)
                m_s[j] = m_new

    need_mask = (ki + 1) * tk - 1 > qi * tq
    for masked in (True, False):
        for fixed in (True, False):
            c_mask = need_mask if masked else jnp.logical_not(need_mask)
            c_fixed = fixed_shift if fixed else jnp.logical_not(fixed_shift)
            pl.when(jnp.logical_and(c_mask, c_fixed))(functools.partial(update, masked, fixed))

    @pl.when(ki == ((qi + 1) * tq - 1) // tk)
    def _():
        if is_diff:
            lam = _diff_lambda(lq1, lk1, lq2, lk2)
            a1 = acc_s[0]
            a2 = acc_s[1]
            od = a1[0:128, :] / a1[128:129, :] - lam * (a2[0:128, :] / a2[128:129, :])
            ms = jnp.mean(od * od, axis=0, keepdims=True)
            y = (od * lax.rsqrt(ms + NORM_EPS)) * gs_ref[...] * (1.0 - LAMBDA_INIT)
            og_ref[...] = (y * gz_ref[...].astype(F32)).astype(BF16)
        else:
            for j in range(2):
                a = acc_s[j]
                o = a[0:64, :] / a[64:65, :]
                og_ref[j * 64:(j + 1) * 64, :] = (o * gz_ref[j * 64:(j + 1) * 64, :].astype(F32)).astype(BF16)


def _causal_pairs(n_tok, tq, tk):
    qi_l, ki_l = [], []
    for qi in range(n_tok // tq):
        for ki in range(((qi + 1) * tq - 1) // tk + 1):
            qi_l.append(qi)
            ki_l.append(ki)
    return jnp.asarray(qi_l, jnp.int32), jnp.asarray(ki_l, jnp.int32)


def _prompt_attention(is_diff, qT, k, vT, gzT, g_q, g_k, extra, n_tok, tq, tk):
    dva = DIFF_DVA if is_diff else FOX_DVA
    v_rows = dva if is_diff else 2 * dva
    qi_tab, ki_tab = _causal_pairs(n_tok, tq, tk)
    n_pairs = qi_tab.shape[0]
    bound = QK_SCALE * HEAD_DIM * jnp.max(jnp.abs(g_q)) * jnp.max(jnp.abs(g_k))
    fixed_tab = (bound <= SHIFT_LIMIT).astype(jnp.int32).reshape(1)
    imap = lambda f: (lambda h, p, qt, kt, fx: f(h, qt[p], kt[p]))
    in_specs = [
        pl.BlockSpec((2 * AUG_DK, tq), imap(lambda h, q, kk: (h, q))),
        pl.BlockSpec((tk, 2 * AUG_DK), imap(lambda h, q, kk: (kk, h))),
        pl.BlockSpec((v_rows, tk), imap(lambda h, q, kk: (h, kk))),
        pl.BlockSpec((128, tq), imap(lambda h, q, kk: (h, q))),
    ] + [pl.BlockSpec(e.shape, imap(lambda h, q, kk: (0, 0))) for e in extra]
    grid_spec = pltpu.PrefetchScalarGridSpec(
        num_scalar_prefetch=3,
        grid=(4, n_pairs),
        in_specs=in_specs,
        out_specs=pl.BlockSpec((128, tq), imap(lambda h, q, kk: (h, q))),
        scratch_shapes=[pltpu.VMEM((2, 1, tq), F32), pltpu.VMEM((2, dva, tq), F32)],
    )
    return pl.pallas_call(
        functools.partial(_attn_body, is_diff, tq, tk, dva),
        grid_spec=grid_spec,
        out_shape=jax.ShapeDtypeStruct((512, n_tok), BF16),
        compiler_params=pltpu.CompilerParams(dimension_semantics=("arbitrary", "arbitrary"),
                                             vmem_limit_bytes=VMEM_LIMIT),
        name="attn_diff" if is_diff else "attn_fox",
    )(qi_tab, ki_tab, fixed_tab, qT, k, vT, gzT, *extra)


def _sample_body(pg, ng, *refs):
    idx = 1
    fk_p = refs[idx:idx + pg]; idx += pg
    fv_p = refs[idx:idx + pg]; idx += pg
    lf_p = refs[idx:idx + pg]; idx += pg
    dk_p = refs[idx:idx + pg]; idx += pg
    dv_p = refs[idx:idx + pg]; idx += pg
    (qf_ref, qd_ref, kfn_ref, vfn_ref, lfn_ref, kdn_ref, vdn_ref, gzf_ref, gzd_ref,
     lq1, lk1, lq2, lk2, gs_ref) = refs[idx:idx + 14]
    idx += 14
    og_ref = refs[idx]
    (qbf_s, qbd_s, mf_s, lf_s, accf_s, md_s, ld_s, accd_s, off_s,
     nk_s, nv_s, nl_s, ndk_s, ndv_s) = refs[idx + 1:]
    g = pl.program_id(1)

    row = lax.broadcasted_iota(jnp.int32, (64, 512), 0)
    col = lax.broadcasted_iota(jnp.int32, (64, 512), 1)
    blockdiag = (row // 8) == (col // 64)

    @pl.when(g == 0)
    def _():
        qf = jnp.concatenate([qf_ref[0]] * 8, axis=0)
        qd = jnp.concatenate([qd_ref[0]] * 8, axis=0)
        qbf_s[...] = jnp.where(blockdiag, qf, 0.0).astype(BF16)
        qbd_s[...] = jnp.where(blockdiag, qd, 0.0).astype(BF16)
        mf_s[...] = jnp.full(mf_s.shape, NEG, F32)
        md_s[...] = jnp.full(md_s.shape, NEG, F32)
        lf_s[...] = jnp.zeros(lf_s.shape, F32)
        ld_s[...] = jnp.zeros(ld_s.shape, F32)
        accf_s[...] = jnp.zeros(accf_s.shape, F32)
        accd_s[...] = jnp.zeros(accd_s.shape, F32)
        off_s[...] = jnp.zeros(off_s.shape, F32)

    def attend(pages, mask):
        n = len(pages)
        cat = lambda xs, axis: xs[0] if n == 1 else jnp.concatenate(xs, axis=axis)
        cw = _lane_cumsum(cat([pg_[2] for pg_ in pages], 0))
        off = off_s[:, 0:1]
        cs = []
        for i in range(n):
            cwi = cw[i * 8:(i + 1) * 8, :]
            c = (off + cwi) * LOG2E
            cs.append(jnp.broadcast_to(c[:, None, :], (8, 8, 128)).reshape(64, 128))
            off = off + cwi[:, 127:128]
        off_s[...] = jnp.broadcast_to(off, off_s.shape)

        kf = cat([pg_[0].astype(BF16) for pg_ in pages], 1)
        s_f = jnp.dot(qbf_s[...], kf, preferred_element_type=F32) - cat(cs, 1)
        kd = cat([pg_[3].astype(BF16) for pg_ in pages], 1)
        s_d = jnp.dot(qbd_s[...], kd, preferred_element_type=F32)
        if mask is not None:
            s_f = jnp.where(mask, s_f, NEG)
            s_d = jnp.where(mask, s_d, NEG)

        m_old = mf_s[...]
        m_new = jnp.maximum(m_old, jnp.max(s_f, axis=1, keepdims=True))
        alpha = jnp.exp2(m_old - m_new)
        pr = jnp.exp2(s_f - m_new)
        lf_s[...] = alpha * lf_s[...] + jnp.sum(pr, axis=1, keepdims=True)
        mf_s[...] = m_new
        vf = cat([pg_[1].astype(BF16) for pg_ in pages], 1)
        pv = lax.dot_general(pr.astype(BF16), vf, (((1,), (1,)), ((), ())), preferred_element_type=F32)
        accf_s[...] = alpha * accf_s[...] + pv

        m_old = md_s[...]
        m_new = jnp.maximum(m_old, jnp.max(s_d, axis=1, keepdims=True))
        alpha = jnp.exp2(m_old - m_new)
        pr = jnp.exp2(s_d - m_new)
        ld_s[...] = alpha * ld_s[...] + jnp.sum(pr, axis=1, keepdims=True)
        md_s[...] = m_new
        pb = pr.astype(BF16)
        for h in range(DIFF_HEADS):
            rows = slice(h * 16, (h + 1) * 16)
            vh = cat([pg_[4][pl.ds(h, 128, stride=4), :].astype(BF16) for pg_ in pages], 0)
            pv = jnp.dot(pb[rows, :], vh, preferred_element_type=F32)
            accd_s[rows, :] = alpha[rows, :] * accd_s[rows, :] + pv

    attend([(fk_p[i][0], fv_p[i][0], lf_p[i][0], dk_p[i][0], dv_p[i].at[0]) for i in range(pg)], None)

    @pl.when(g == ng - 1)
    def _():
        nk_s[...] = jnp.zeros(nk_s.shape, F32)
        nv_s[...] = jnp.zeros(nv_s.shape, F32)
        nl_s[...] = jnp.zeros(nl_s.shape, F32)
        ndk_s[...] = jnp.zeros(ndk_s.shape, F32)
        ndv_s[...] = jnp.zeros(ndv_s.shape, F32)
        nk_s[:, 0:8] = kfn_ref[0]
        nv_s[:, 0:8] = vfn_ref[0]
        nl_s[:, 0:8] = lfn_ref[0]
        ndk_s[:, 0:8] = kdn_ref[0]
        ndv_s[0:32, :] = vdn_ref[0]
        r = lax.broadcasted_iota(jnp.int32, (64, 128), 0)
        c = lax.broadcasted_iota(jnp.int32, (64, 128), 1)
        attend([(nk_s[...], nv_s[...], nl_s[...], ndk_s[...], ndv_s)], c <= (r & 7))

        of = jnp.where(blockdiag, accf_s[...] / lf_s[...], 0.0)
        of = jnp.sum(of.reshape(8, 8, 512), axis=0)
        ogf = of * gzf_ref[0]
        lam = _diff_lambda(lq1, lk1, lq2, lk2)
        on = accd_s[...] / ld_s[...]
        outs = [ogf]
        for h in range(DIFF_HEADS):
            od = on[h * 16:h * 16 + 8, :] - lam * on[h * 16 + 8:h * 16 + 16, :]
            ms = jnp.mean(od * od, axis=-1, keepdims=True)
            y = (od * lax.rsqrt(ms + NORM_EPS)) * gs_ref[...] * (1.0 - LAMBDA_INIT)
            outs.append(y * gzd_ref[0][:, h * 128:(h + 1) * 128])
        og_ref[0] = jnp.concatenate(outs, axis=1).astype(BF16)


def _sample_attention(page_table, pools, per_seq, extra, pg):
    n_seq, n_pages = page_table.shape
    ng = n_pages // pg
    pt_flat = page_table.reshape(-1)
    fk_pool, fv_pool, lf_pool, dk_pool, dv_pool = pools

    def page_spec(shape, i):
        return pl.BlockSpec((1,) + shape, lambda b, g, pt: (pt[b * n_pages + g * pg + i], 0, 0))

    in_specs = []
    operands = []
    for pool, shape in ((fk_pool, (512, 128)), (fv_pool, (512, 128)), (lf_pool, (8, 128)),
                        (dk_pool, (512, 128)), (dv_pool, (512, 128))):
        for i in range(pg):
            in_specs.append(page_spec(shape, i))
            operands.append(pool)
    for a in per_seq:
        in_specs.append(pl.BlockSpec((1,) + a.shape[1:], lambda b, g, pt: (b, 0, 0)))
        operands.append(a)
    for e in extra:
        in_specs.append(pl.BlockSpec(e.shape, lambda b, g, pt: (0, 0)))
        operands.append(e)
    scratch = [
        pltpu.VMEM((64, 512), BF16), pltpu.VMEM((64, 512), BF16),
        pltpu.VMEM((64, 1), F32), pltpu.VMEM((64, 1), F32), pltpu.VMEM((64, 512), F32),
        pltpu.VMEM((64, 1), F32), pltpu.VMEM((64, 1), F32), pltpu.VMEM((64, 128), F32),
        pltpu.VMEM((8, 128), F32),
        pltpu.VMEM((512, 128), F32), pltpu.VMEM((512, 128), F32), pltpu.VMEM((8, 128), F32),
        pltpu.VMEM((512, 128), F32), pltpu.VMEM((512, 128), F32),
    ]
    grid_spec = pltpu.PrefetchScalarGridSpec(
        num_scalar_prefetch=1,
        grid=(n_seq, ng),
        in_specs=in_specs,
        out_specs=pl.BlockSpec((1, 8, 1024), lambda b, g, pt: (b, 0, 0)),
        scratch_shapes=scratch,
    )
    return pl.pallas_call(
        functools.partial(_sample_body, pg, ng),
        grid_spec=grid_spec,
        out_shape=jax.ShapeDtypeStruct((n_seq, 8, 1024), BF16),
        compiler_params=pltpu.CompilerParams(dimension_semantics=("arbitrary", "arbitrary"),
                                             vmem_limit_bytes=VMEM_LIMIT),
        name="attn_sample",
    )(pt_flat, *operands)


def _out_t_body(x_ref, of_ref, od_ref, woT_ref, o_ref):
    y_t = jnp.dot(woT_ref[:, 0:512], of_ref[...], preferred_element_type=F32)
    y_t = y_t + jnp.dot(woT_ref[:, 512:1024], od_ref[...], preferred_element_type=F32)
    o_ref[...] = x_ref[...] + y_t.T


def _out_proj_t(x2d, ogf_t, ogd_t, wo_t, tm):
    n_tok = x2d.shape[0]
    return pl.pallas_call(
        _out_t_body,
        grid=(n_tok // tm,),
        in_specs=[pl.BlockSpec((tm, D_MODEL), lambda i: (i, 0)),
                  pl.BlockSpec((512, tm), lambda i: (0, i)),
                  pl.BlockSpec((512, tm), lambda i: (0, i)),
                  pl.BlockSpec((D_MODEL, D_MODEL), lambda i: (0, 0))],
        out_specs=pl.BlockSpec((tm, D_MODEL), lambda i: (i, 0)),
        out_shape=jax.ShapeDtypeStruct((n_tok, D_MODEL), F32),
        compiler_params=pltpu.CompilerParams(dimension_semantics=("arbitrary",),
                                             vmem_limit_bytes=VMEM_LIMIT),
        name="out_proj_prompt",
    )(x2d, ogf_t, ogd_t, wo_t)


def _out_body(x_ref, og_ref, wo_ref, o_ref):
    o_ref[...] = x_ref[...] + jnp.dot(og_ref[...], wo_ref[...], preferred_element_type=F32)


def _out_proj(x2d, og, wo, tm):
    n_tok = x2d.shape[0]
    return pl.pallas_call(
        _out_body,
        grid=(n_tok // tm,),
        in_specs=[pl.BlockSpec((tm, D_MODEL), lambda i: (i, 0)),
                  pl.BlockSpec((tm, D_MODEL), lambda i: (i, 0)),
                  pl.BlockSpec((D_MODEL, D_MODEL), lambda i: (0, 0))],
        out_specs=pl.BlockSpec((tm, D_MODEL), lambda i: (i, 0)),
        out_shape=jax.ShapeDtypeStruct((n_tok, D_MODEL), F32),
        compiler_params=pltpu.CompilerParams(dimension_semantics=("arbitrary",),
                                             vmem_limit_bytes=VMEM_LIMIT),
        name="out_proj_sample",
    )(x2d, og, wo)


def kernel(x_prompt, x_sample, cache_fox_k, cache_fox_v, cache_fox_logf, cache_diff_k, cache_diff_v,
           page_table, g_norm, w_in, b_f, g_fox_q, g_fox_k, g_diff_q, g_diff_k,
           lam_q1, lam_k1, lam_q2, lam_k2, g_subln, w_out):
    seq = x_prompt.shape[1]
    n_seq, dec = x_sample.shape[0], x_sample.shape[1]
    past = page_table.shape[1] * PAGE_SIZE
    n_pool = cache_fox_k.shape[1]

    w = w_in[0]
    sizes = [FOX_WIDTH, FOX_WIDTH, FOX_WIDTH, FOX_HEADS, FOX_WIDTH, DIFF_QK_WIDTH, DIFF_QK_WIDTH,
             DIFF_WIDTH, DIFF_WIDTH]
    offs = [0]
    for s in sizes:
        offs.append(offs[-1] + s)
    col = lambda k: w[:, offs[k]:offs[k + 1]]
    w_t = jnp.concatenate(
        [col(0), col(1), col(2), col(4), col(5), col(6), col(7), col(8), col(3),
         jnp.zeros((D_MODEL, 8), w.dtype)], axis=1).T.astype(BF16)
    w_dv = col(7).astype(BF16)
    half = ROT_DIM // 2
    inv = (ROPE_THETA ** (-jnp.arange(half, dtype=F32) * 2.0 / ROT_DIM)).reshape(half, 1)
    colv = lambda a: a[0].astype(F32).reshape(-1, 1)
    params = (g_norm.astype(F32), w_t, w_dv, colv(b_f), colv(g_fox_q), colv(g_fox_k),
              colv(g_diff_q), colv(g_diff_k), inv)
    lam_params = (lam_q1.astype(F32), lam_k1.astype(F32), lam_q2.astype(F32), lam_k2.astype(F32))
    wo = w_out[0].astype(BF16)

    xp2 = x_prompt[0]
    (fk_t, fv_t, lf_t, dk_t, dv_p, qfa_t, kfa, vfa_t, gzf_t, qd_t, kd, vda_t, gzd_t) = _project(
        xp2, params, True, 256, 0, 256, 1)
    tq = tk = 1024
    ogf_t = _prompt_attention(False, qfa_t, kfa, vfa_t, gzf_t, g_fox_q, g_fox_k, (), seq, tq, tk)
    ogd_t = _prompt_attention(True, qd_t, kd, vda_t, gzd_t, g_diff_q, g_diff_k,
                              lam_params + (colv(g_subln),), seq, tq, tk)
    y_prompt = _out_proj_t(xp2, ogf_t, ogd_t, wo.T, 256)[None]

    xs_tb = jnp.transpose(x_sample, (1, 0, 2)).reshape(dec * n_seq, D_MODEL)
    (sfk, sfv, slf, sdk, sdv, sfq, sdq, sgzf, sgzd) = _project(
        xs_tb, params, False, n_seq, past, 1, 0)
    bt = lambda a: jnp.transpose(a, (2, 0, 1))
    bf = lambda a: jnp.transpose(a, (2, 1, 0))
    per_seq = (bt(sfq), bt(sdq), bf(sfk), bf(sfv), bf(slf), bf(sdk),
               jnp.transpose(sdv, (1, 0, 2)).reshape(n_seq, dec * DIFF_HEADS, DIFF_V_DIM),
               bt(sgzf), bt(sgzd))
    pools = (
        jnp.transpose(cache_fox_k[0], (0, 2, 3, 1)).reshape(n_pool, 512, PAGE_SIZE),
        jnp.transpose(cache_fox_v[0], (0, 2, 3, 1)).reshape(n_pool, 512, PAGE_SIZE),
        jnp.transpose(cache_fox_logf[0], (0, 2, 1)),
        jnp.transpose(cache_diff_k[0], (0, 2, 3, 4, 1)).reshape(n_pool, 512, PAGE_SIZE),
        cache_diff_v[0].reshape(n_pool, PAGE_SIZE * DIFF_HEADS, DIFF_V_DIM),
    )
    og_s = _sample_attention(page_table, pools, per_seq,
                             lam_params + (g_subln.astype(F32),), 8)
    y_sample = _out_proj(x_sample.reshape(n_seq * dec, D_MODEL), og_s.reshape(n_seq * dec, D_MODEL),
                         wo, 256).reshape(n_seq, dec, D_MODEL)

    new_fk_p = jnp.transpose(fk_t.reshape(FOX_HEADS, HEAD_DIM, seq), (2, 0, 1))[None, None]
    new_fv_p = jnp.transpose(fv_t.reshape(FOX_HEADS, HEAD_DIM, seq), (2, 0, 1))[None, None]
    new_fl_p = jnp.transpose(lf_t, (1, 0))[None, None]
    new_dk_p = jnp.transpose(dk_t.reshape(DIFF_HEADS, 2, DIFF_QK_DIM, seq), (3, 0, 1, 2))[None, None]
    new_dv_p = dv_p.reshape(seq, DIFF_HEADS, DIFF_V_DIM)[None, None]
    new_fk_s = jnp.transpose(sfk.reshape(dec, FOX_HEADS, HEAD_DIM, n_seq), (3, 0, 1, 2))[None]
    new_fv_s = jnp.transpose(sfv.reshape(dec, FOX_HEADS, HEAD_DIM, n_seq), (3, 0, 1, 2))[None]
    new_fl_s = jnp.transpose(slf, (2, 0, 1))[None]
    new_dk_s = jnp.transpose(sdk.reshape(dec, DIFF_HEADS, 2, DIFF_QK_DIM, n_seq), (4, 0, 1, 2, 3))[None]
    new_dv_s = jnp.transpose(sdv, (1, 0, 2)).reshape(n_seq, dec, DIFF_HEADS, DIFF_V_DIM)[None]
    return (y_prompt, y_sample, new_fk_p, new_fv_p, new_fl_p, new_dk_p, new_dv_p,
            new_fk_s, new_fv_s, new_fl_s, new_dk_s, new_dv_s)
```

```python
import functools

import jax
import jax.numpy as jnp
from jax import lax
from jax.experimental import pallas as pl
from jax.experimental.pallas import tpu as pltpu

D_MODEL = 1024
HEAD_DIM = 64
FOX_HEADS = 8
FOX_WIDTH = FOX_HEADS * HEAD_DIM
DIFF_HEADS = 4
DIFF_QK_DIM = 64
DIFF_V_DIM = 2 * DIFF_QK_DIM
DIFF_QK_WIDTH = DIFF_HEADS * 2 * DIFF_QK_DIM
DIFF_WIDTH = DIFF_HEADS * DIFF_V_DIM
PAGE_SIZE = 128
ROPE_THETA = 500000.0
ROT_DIM = DIFF_QK_DIM // 4
NORM_EPS = 1e-6
LAMBDA_INIT = 0.2

LOG2E = 1.4426950408889634
QK_SCALE = LOG2E * HEAD_DIM ** -0.5
NEG = -1e30

F32 = jnp.float32
BF16 = jnp.bfloat16

AUG_DK = 128
SHIFT_LIMIT = 60.0
FOX_DVA = 80
DIFF_DVA = 144

VMEM_LIMIT = 56 * 1024 * 1024
SAMPLE_PAGES_PER_STEP = 8

R_FQ, R_FK, R_FV, R_FZ, R_DQ, R_DK, R_DV, R_DZ, R_FF = (i * 512 for i in range(9))
WT_ROWS = R_FF + 16


def _lane_cumsum(x):
    lane = lax.broadcasted_iota(jnp.int32, x.shape, 1)
    sh = 1
    while sh < x.shape[1]:
        x = x + jnp.where(lane >= sh, pltpu.roll(x, sh, 1), 0.0)
        sh *= 2
    return x


def _silu(z):
    return z * (1.0 / (1.0 + jnp.exp(-z)))


def _log_sigmoid(z):
    return jnp.minimum(z, 0.0) - jnp.log1p(jnp.exp(-jnp.abs(z)))


def _diff_lambda(lq1, lk1, lq2, lk2):
    a = jnp.sum(lq1[...] * lk1[...], axis=-1, keepdims=True)
    b = jnp.sum(lq2[...] * lk2[...], axis=-1, keepdims=True)
    return jnp.exp(a) - jnp.exp(b) + LAMBDA_INIT


def _proj_body(is_prompt, tm, pos_base, pos_tile, pos_lane, *refs):
    (x_ref, gn_ref, wT_ref, wdv_ref, bf_ref, gfq_ref, gfk_ref, gdq_ref, gdk_ref, inv_ref) = refs[:10]
    rest = refs[10:]
    i = pl.program_id(0)

    x = x_ref[...]
    ms = jnp.mean(x * x, axis=-1, keepdims=True)
    xn = (x * lax.rsqrt(ms + NORM_EPS)) * gn_ref[...]
    xb = xn.astype(BF16)

    def proj_t(r0, n):
        return lax.dot_general(wT_ref[r0:r0 + n, :], xb, (((1,), (1,)), ((), ())),
                               preferred_element_type=F32)

    def head_norm(h_t, g_ref):
        y = h_t.reshape(8, 64, tm)
        m = jnp.mean(y * y, axis=1, keepdims=True)
        return (y * lax.rsqrt(m + NORM_EPS)) * g_ref[...][None]

    pos = pos_base + i * pos_tile + lax.broadcasted_iota(jnp.int32, (1, tm), 1) * pos_lane
    ang = pos.astype(F32) * inv_ref[...]
    cos = jnp.cos(ang)[None]
    sin = jnp.sin(ang)[None]

    def rope(y):
        x1 = y[:, 0:8, :]
        x2 = y[:, 8:16, :]
        return jnp.concatenate([x1 * cos - x2 * sin, x2 * cos + x1 * sin, y[:, 16:, :]], axis=1)

    fq = head_norm(proj_t(R_FQ, 512), gfq_ref)
    fk = head_norm(proj_t(R_FK, 512), gfk_ref)
    fv = proj_t(R_FV, 512)
    lf = _log_sigmoid(proj_t(R_FF, 16)[0:8, :] + bf_ref[...])
    gzf = _silu(proj_t(R_FZ, 512))
    dq = rope(head_norm(proj_t(R_DQ, 512), gdq_ref))
    dk = rope(head_norm(proj_t(R_DK, 512), gdk_ref))
    dv_t = proj_t(R_DV, 512)
    gzd = _silu(proj_t(R_DZ, 512))
    dv_rm = jnp.dot(xb, wdv_ref[...], preferred_element_type=F32)

    fk2 = fk.reshape(512, tm)
    dk2 = dk.reshape(512, tm)
    fq2 = fq.reshape(512, tm) * QK_SCALE
    dq2 = dq.reshape(512, tm) * QK_SCALE

    if not is_prompt:
        (fk_o, fv_o, lf_o, dk_o, dv_o, fq_o, dq_o, gzf_o, gzd_o) = rest
        fk_o[...] = fk2
        fv_o[...] = fv
        lf_o[...] = lf
        dk_o[...] = dk2
        dv_o[...] = dv_rm
        fq_o[...] = fq2
        dq_o[...] = dq2
        gzf_o[...] = gzf
        gzd_o[...] = gzd
        return

    (fk_o, fv_o, lf_o, dk_o, dv_o, qfa_o, kfa_o, vfa_o, gzf_o, qd_o, kd_o, vda_o, gzd_o, carry) = rest
    fk_o[...] = fk2
    fv_o[...] = fv
    lf_o[...] = lf
    dk_o[...] = dk2
    for h in range(DIFF_HEADS):
        dv_o[pl.ds(h, tm, stride=DIFF_HEADS), :] = dv_rm[:, h * DIFF_V_DIM:(h + 1) * DIFF_V_DIM]
    gzf_o[...] = gzf.astype(BF16)
    gzd_o[...] = gzd.astype(BF16)

    @pl.when(i == 0)
    def _():
        carry[...] = jnp.zeros_like(carry)

    run = carry[:, 0:1]
    chunks = []
    for j in range(tm // 128):
        cj = run + _lane_cumsum(lf[:, j * 128:(j + 1) * 128])
        run = cj[:, 127:128]
        chunks.append(cj)
    carry[...] = jnp.broadcast_to(run, carry.shape)
    c2 = jnp.concatenate(chunks, axis=1) * LOG2E

    def split3(v):
        hi = v.astype(BF16).astype(F32)
        r1 = v - hi
        mid = r1.astype(BF16).astype(F32)
        return hi, mid, (r1 - mid).astype(BF16).astype(F32)

    rowid = lax.broadcasted_iota(jnp.int32, (8, tm), 0)

    def slab(pieces, h):
        return jnp.where(rowid == 0, pieces[0][h:h + 1, :],
                         jnp.where(rowid == 1, pieces[1][h:h + 1, :],
                                   jnp.where(rowid == 2, pieces[2][h:h + 1, :], 0.0)))

    def shift_bound(q2, g_ref):
        n = jnp.sqrt(jnp.sum(jnp.square(q2.reshape(8, 64, tm)), axis=1))
        return n * (8.0 * jnp.max(jnp.abs(g_ref[...]), axis=0, keepdims=True))

    c_pieces = split3(c2)
    mf_pieces = split3(shift_bound(fq2, gfk_ref))
    md_pieces = split3(shift_bound(dq2, gdk_ref))
    ones3 = jnp.where(rowid < 3, 1.0, 0.0)
    zeros8 = jnp.zeros((8, tm), F32)
    zeros40 = jnp.zeros((40, tm), F32)
    one_row16 = jnp.where(lax.broadcasted_iota(jnp.int32, (16, tm), 0) == 0, 1.0, 0.0)

    for h in range(FOX_HEADS):
        cs = slab(c_pieces, h)
        q_aug = jnp.concatenate([fq2[h * 64:(h + 1) * 64, :], cs, ones3, -slab(mf_pieces, h), zeros40], axis=0)
        k_aug = jnp.concatenate([fk2[h * 64:(h + 1) * 64, :], ones3, -cs, ones3, zeros40], axis=0)
        qfa_o[h * AUG_DK:(h + 1) * AUG_DK, :] = q_aug.astype(BF16)
        kfa_o[:, h * AUG_DK:(h + 1) * AUG_DK] = k_aug.T.astype(BF16)
        v_aug = jnp.concatenate([fv[h * 64:(h + 1) * 64, :], one_row16], axis=0)
        vfa_o[h * FOX_DVA:(h + 1) * FOX_DVA, :] = v_aug.astype(BF16)

    for j in range(2 * DIFF_HEADS):
        q_aug = jnp.concatenate([dq2[j * 64:(j + 1) * 64, :], zeros8, zeros8, -slab(md_pieces, j), zeros40], axis=0)
        k_aug = jnp.concatenate([dk2[j * 64:(j + 1) * 64, :], zeros8, zeros8, ones3, zeros40], axis=0)
        qd_o[j * AUG_DK:(j + 1) * AUG_DK, :] = q_aug.astype(BF16)
        kd_o[:, j * AUG_DK:(j + 1) * AUG_DK] = k_aug.T.astype(BF16)

    for h in range(DIFF_HEADS):
        v_aug = jnp.concatenate([dv_t[h * 128:(h + 1) * 128, :], one_row16], axis=0)
        vda_o[h * DIFF_DVA:(h + 1) * DIFF_DVA, :] = v_aug.astype(BF16)


def _project(x2d, params, is_prompt, tm, pos_base, pos_tile, pos_lane):
    n_tok = x2d.shape[0]
    n_tiles = n_tok // tm
    full = lambda a: pl.BlockSpec(a.shape, lambda i: (0,) * a.ndim)
    in_specs = [pl.BlockSpec((tm, D_MODEL), lambda i: (i, 0))] + [full(p) for p in params]
    t_spec = lambda rows: pl.BlockSpec((rows, tm), lambda i: (0, i))
    r_spec = lambda cols: pl.BlockSpec((tm, cols), lambda i: (i, 0))
    if is_prompt:
        sds = jax.ShapeDtypeStruct
        out_shape = [
            sds((512, n_tok), F32), sds((512, n_tok), F32), sds((8, n_tok), F32), sds((512, n_tok), F32),
            sds((DIFF_HEADS * n_tok, DIFF_V_DIM), F32),
            sds((8 * AUG_DK, n_tok), BF16), sds((n_tok, 8 * AUG_DK), BF16),
            sds((FOX_HEADS * FOX_DVA, n_tok), BF16), sds((512, n_tok), BF16),
            sds((8 * AUG_DK, n_tok), BF16), sds((n_tok, 8 * AUG_DK), BF16),
            sds((DIFF_HEADS * DIFF_DVA, n_tok), BF16), sds((512, n_tok), BF16),
        ]
        out_specs = [
            t_spec(512), t_spec(512), t_spec(8), t_spec(512),
            pl.BlockSpec((DIFF_HEADS * tm, DIFF_V_DIM), lambda i: (i, 0)),
            t_spec(8 * AUG_DK), r_spec(8 * AUG_DK),
            t_spec(FOX_HEADS * FOX_DVA), t_spec(512),
            t_spec(8 * AUG_DK), r_spec(8 * AUG_DK), t_spec(DIFF_HEADS * DIFF_DVA), t_spec(512),
        ]
        scratch = [pltpu.VMEM((8, 128), F32)]
    else:
        sds = jax.ShapeDtypeStruct
        s3 = lambda rows: pl.BlockSpec((None, rows, tm), lambda i: (i, 0, 0))
        out_shape = [
            sds((n_tiles, 512, tm), F32), sds((n_tiles, 512, tm), F32), sds((n_tiles, 8, tm), F32),
            sds((n_tiles, 512, tm), F32), sds((n_tiles, tm, 512), F32),
            sds((n_tiles, 512, tm), F32), sds((n_tiles, 512, tm), F32),
            sds((n_tiles, 512, tm), F32), sds((n_tiles, 512, tm), F32),
        ]
        out_specs = [s3(512), s3(512), s3(8), s3(512),
                     pl.BlockSpec((None, tm, 512), lambda i: (i, 0, 0)),
                     s3(512), s3(512), s3(512), s3(512)]
        scratch = []
    return pl.pallas_call(
        functools.partial(_proj_body, is_prompt, tm, pos_base, pos_tile, pos_lane),
        grid=(n_tiles,),
        in_specs=in_specs,
        out_specs=out_specs,
        out_shape=out_shape,
        scratch_shapes=scratch,
        compiler_params=pltpu.CompilerParams(dimension_semantics=("arbitrary",),
                                             vmem_limit_bytes=VMEM_LIMIT),
        name="proj_prompt" if is_prompt else "proj_sample",
    )(x2d, *params)


def _attn_step(is_diff, tq, tk, dva, qi, ki, fixed_shift, in_refs, og_ref, m_s, acc_s):
    qT_ref, k_ref, v_ref, gz_ref = in_refs[:4]
    if is_diff:
        lq1, lk1, lq2, lk2, gs_ref = in_refs[4:]
    dk = AUG_DK

    @pl.when(ki == 0)
    def _():
        m_s[...] = jnp.full(m_s.shape, NEG, F32)
        acc_s[...] = jnp.zeros(acc_s.shape, F32)

    def update(masked, fixed):
        for j in range(2):
            s = jnp.dot(k_ref[:, j * dk:(j + 1) * dk], qT_ref[j * dk:(j + 1) * dk, :],
                        preferred_element_type=F32)
            if masked:
                kpos = ki * tk + lax.broadcasted_iota(jnp.int32, (tk, tq), 0)
                qpos = qi * tq + lax.broadcasted_iota(jnp.int32, (tk, tq), 1)
                s = jnp.where(kpos <= qpos, s, NEG)
            vj = v_ref[...] if is_diff else v_ref[j * dva:(j + 1) * dva, :]
            if fixed:
                acc_s[j] += jnp.dot(vj, jnp.exp2(s).astype(BF16), preferred_element_type=F32)
            else:
                m_old = m_s[j]
                m_new = jnp.maximum(m_old, jnp.max(s, axis=0, keepdims=True))
                alpha = jnp.exp2(m_old - m_new)
                pb = jnp.exp2(s - m_new).astype(BF16)
                acc_s[j] = alpha * acc_s[j] + jnp.dot(vj, pb, preferred_element_type=F32)
                m_s[j] = m_new

    need_mask = (ki + 1) * tk - 1 > qi * tq
    for masked in (True, False):
        for fixed in (True, False):
            c_mask = need_mask if masked else jnp.logical_not(need_mask)
            c_fixed = fixed_shift if fixed else jnp.logical_not(fixed_shift)
            pl.when(jnp.logical_and(c_mask, c_fixed))(functools.partial(update, masked, fixed))

    @pl.when(ki == ((qi + 1) * tq - 1) // tk)
    def _():
        if is_diff:
            lam = _diff_lambda(lq1, lk1, lq2, lk2)
            a1 = acc_s[0]
            a2 = acc_s[1]
            od = a1[0:128, :] / a1[128:129, :] - lam * (a2[0:128, :] / a2[128:129, :])
            ms = jnp.mean(od * od, axis=0, keepdims=True)
            y = (od * lax.rsqrt(ms + NORM_EPS)) * gs_ref[...] * (1.0 - LAMBDA_INIT)
            og_ref[...] = (y * gz_ref[...].astype(F32)).astype(BF16)
        else:
            for j in range(2):
                a = acc_s[j]
                o = a[0:64, :] / a[64:65, :]
                og_ref[j * 64:(j + 1) * 64, :] = (o * gz_ref[j * 64:(j + 1) * 64, :].astype(F32)).astype(BF16)


def _causal_pairs(n_tok, tq, tk):
    qi_l, ki_l = [], []
    for qi in range(n_tok // tq):
        for ki in range(((qi + 1) * tq - 1) // tk + 1):
            qi_l.append(qi)
            ki_l.append(ki)
    return jnp.asarray(qi_l, jnp.int32), jnp.asarray(ki_l, jnp.int32)


def _fused_body(is_diff, tq, tk, dva, pg, ng, n_pairs, n_steps, n_sample_steps, n_prompt_in, *refs):
    qi_tab, ki_tab, fixed_tab = refs[:3]
    refs = refs[4:]
    n_sample_in = 5 * pg + N_SAMPLE_SEQ_INPUTS + 5
    prompt_in = refs[:n_prompt_in]
    sample_in = refs[n_prompt_in:n_prompt_in + n_sample_in]
    og_ref, ogs_ref = refs[n_prompt_in + n_sample_in:n_prompt_in + n_sample_in + 2]
    scratch = refs[n_prompt_in + n_sample_in + 2:]
    m_s, acc_s = scratch[:2]
    p = pl.program_id(1)
    step = pl.program_id(0) * n_steps + p

    @pl.when(p < n_pairs)
    def _():
        pp = jnp.minimum(p, n_pairs - 1)
        _attn_step(is_diff, tq, tk, dva, qi_tab[pp], ki_tab[pp], fixed_tab[0] == 1,
                   prompt_in, og_ref, m_s, acc_s)

    @pl.when(step < n_sample_steps)
    def _():
        _sample_step(pg, ng, step % ng, sample_in, ogs_ref, scratch[2:])


def _fused_attention(is_diff, qT, k, vT, gzT, g_q, g_k, prompt_extra, n_tok, tq, tk,
                     page_table, seq0, n_seq_call, pools, per_seq, sample_extra, pg):
    dva = DIFF_DVA if is_diff else FOX_DVA
    v_rows = dva if is_diff else 2 * dva
    qi_tab, ki_tab = _causal_pairs(n_tok, tq, tk)
    n_pairs = qi_tab.shape[0]
    n_pages = page_table.shape[1]
    ng = n_pages // pg
    n_sample_steps = n_seq_call * ng
    n_steps = max(n_pairs, -(-n_sample_steps // 4))
    bound = QK_SCALE * HEAD_DIM * jnp.max(jnp.abs(g_q)) * jnp.max(jnp.abs(g_k))
    fixed_tab = (bound <= SHIFT_LIMIT).astype(jnp.int32).reshape(1)
    pt_flat = page_table.reshape(-1)

    def pmap(f):
        def index_map(h, p, qt, kt, fx, pt):
            pp = jnp.minimum(p, n_pairs - 1)
            return f(h, qt[pp], kt[pp])
        return index_map

    def sample_pos(h, p):
        step = jnp.minimum(h * n_steps + p, n_sample_steps - 1)
        return seq0 + step // ng, step % ng

    def page_map(i):
        def index_map(h, p, qt, kt, fx, pt):
            b, g = sample_pos(h, p)
            return (pt[b * n_pages + g * pg + i], 0, 0)
        return index_map

    def seq_map(h, p, qt, kt, fx, pt):
        return (sample_pos(h, p)[0], 0, 0)

    def out_seq_map(h, p, qt, kt, fx, pt):
        return (sample_pos(h, p)[0] - seq0, 0, 0)

    in_specs = [
        pl.BlockSpec((2 * AUG_DK, tq), pmap(lambda h, q, kk: (h, q))),
        pl.BlockSpec((tk, 2 * AUG_DK), pmap(lambda h, q, kk: (kk, h))),
        pl.BlockSpec((v_rows, tk), pmap(lambda h, q, kk: (h, kk))),
        pl.BlockSpec((128, tq), pmap(lambda h, q, kk: (h, q))),
    ] + [pl.BlockSpec(e.shape, pmap(lambda h, q, kk: (0, 0))) for e in prompt_extra]
    operands = [qT, k, vT, gzT, *prompt_extra]
    n_prompt_in = len(operands)
    for pool, shape in zip(pools, ((512, 128), (512, 128), (8, 128), (512, 128), (512, 128))):
        for i in range(pg):
            in_specs.append(pl.BlockSpec((1,) + shape, page_map(i)))
            operands.append(pool)
    assert len(per_seq) == N_SAMPLE_SEQ_INPUTS
    for a in per_seq:
        in_specs.append(pl.BlockSpec((1,) + a.shape[1:], seq_map))
        operands.append(a)
    for e in sample_extra:
        in_specs.append(pl.BlockSpec(e.shape, pmap(lambda h, q, kk: (0, 0))))
        operands.append(e)
    scratch = [
        pltpu.VMEM((2, 1, tq), F32), pltpu.VMEM((2, dva, tq), F32),
        pltpu.VMEM((64, 512), BF16), pltpu.VMEM((64, 512), BF16),
        pltpu.VMEM((64, 1), F32), pltpu.VMEM((64, 1), F32), pltpu.VMEM((64, 512), F32),
        pltpu.VMEM((64, 1), F32), pltpu.VMEM((64, 1), F32), pltpu.VMEM((64, 128), F32),
        pltpu.VMEM((8, 128), F32),
        pltpu.VMEM((512, 128), F32), pltpu.VMEM((512, 128), F32), pltpu.VMEM((8, 128), F32),
        pltpu.VMEM((512, 128), F32), pltpu.VMEM((512, 128), F32),
    ]
    assert len(scratch) == 2 + N_SAMPLE_SCRATCH
    grid_spec = pltpu.PrefetchScalarGridSpec(
        num_scalar_prefetch=4,
        grid=(4, n_steps),
        in_specs=in_specs,
        out_specs=[pl.BlockSpec((128, tq), pmap(lambda h, q, kk: (h, q))),
                   pl.BlockSpec((1, 8, 1024), out_seq_map)],
        scratch_shapes=scratch,
    )
    return pl.pallas_call(
        functools.partial(_fused_body, is_diff, tq, tk, dva, pg, ng, n_pairs, n_steps,
                          n_sample_steps, n_prompt_in),
        grid_spec=grid_spec,
        out_shape=[jax.ShapeDtypeStruct((512, n_tok), BF16),
                   jax.ShapeDtypeStruct((n_seq_call, 8, 1024), BF16)],
        compiler_params=pltpu.CompilerParams(dimension_semantics=("arbitrary", "arbitrary"),
                                             vmem_limit_bytes=VMEM_LIMIT),
        name="attn_diff" if is_diff else "attn_fox",
    )(qi_tab, ki_tab, fixed_tab, pt_flat, *operands)


N_SAMPLE_SEQ_INPUTS = 9
N_SAMPLE_SCRATCH = 14


def _sample_step(pg, ng, g, in_refs, og_ref, scratch):
    fk_p = in_refs[0 * pg:1 * pg]
    fv_p = in_refs[1 * pg:2 * pg]
    lf_p = in_refs[2 * pg:3 * pg]
    dk_p = in_refs[3 * pg:4 * pg]
    dv_p = in_refs[4 * pg:5 * pg]
    (qf_ref, qd_ref, kfn_ref, vfn_ref, lfn_ref, kdn_ref, vdn_ref, gzf_ref, gzd_ref,
     lq1, lk1, lq2, lk2, gs_ref) = in_refs[5 * pg:]
    (qbf_s, qbd_s, mf_s, lf_s, accf_s, md_s, ld_s, accd_s, off_s,
     nk_s, nv_s, nl_s, ndk_s, ndv_s) = scratch

    row = lax.broadcasted_iota(jnp.int32, (64, 512), 0)
    col = lax.broadcasted_iota(jnp.int32, (64, 512), 1)
    blockdiag = (row // 8) == (col // 64)

    @pl.when(g == 0)
    def _():
        qf = jnp.concatenate([qf_ref[0]] * 8, axis=0)
        qd = jnp.concatenate([qd_ref[0]] * 8, axis=0)
        qbf_s[...] = jnp.where(blockdiag, qf, 0.0).astype(BF16)
        qbd_s[...] = jnp.where(blockdiag, qd, 0.0).astype(BF16)
        mf_s[...] = jnp.full(mf_s.shape, NEG, F32)
        md_s[...] = jnp.full(md_s.shape, NEG, F32)
        lf_s[...] = jnp.zeros(lf_s.shape, F32)
        ld_s[...] = jnp.zeros(ld_s.shape, F32)
        accf_s[...] = jnp.zeros(accf_s.shape, F32)
        accd_s[...] = jnp.zeros(accd_s.shape, F32)
        off_s[...] = jnp.zeros(off_s.shape, F32)

    def attend(pages, mask):
        n = len(pages)
        cat = lambda xs, axis: xs[0] if n == 1 else jnp.concatenate(xs, axis=axis)
        cw = _lane_cumsum(cat([pg_[2] for pg_ in pages], 0))
        off = off_s[:, 0:1]
        cs = []
        for i in range(n):
            cwi = cw[i * 8:(i + 1) * 8, :]
            c = (off + cwi) * LOG2E
            cs.append(jnp.broadcast_to(c[:, None, :], (8, 8, 128)).reshape(64, 128))
            off = off + cwi[:, 127:128]
        off_s[...] = jnp.broadcast_to(off, off_s.shape)

        kf = cat([pg_[0].astype(BF16) for pg_ in pages], 1)
        s_f = jnp.dot(qbf_s[...], kf, preferred_element_type=F32) - cat(cs, 1)
        kd = cat([pg_[3].astype(BF16) for pg_ in pages], 1)
        s_d = jnp.dot(qbd_s[...], kd, preferred_element_type=F32)
        if mask is not None:
            s_f = jnp.where(mask, s_f, NEG)
            s_d = jnp.where(mask, s_d, NEG)

        m_old = mf_s[...]
        m_new = jnp.maximum(m_old, jnp.max(s_f, axis=1, keepdims=True))
        alpha = jnp.exp2(m_old - m_new)
        pr = jnp.exp2(s_f - m_new)
        lf_s[...] = alpha * lf_s[...] + jnp.sum(pr, axis=1, keepdims=True)
        mf_s[...] = m_new
        vf = cat([pg_[1].astype(BF16) for pg_ in pages], 1)
        pv = lax.dot_general(pr.astype(BF16), vf, (((1,), (1,)), ((), ())), preferred_element_type=F32)
        accf_s[...] = alpha * accf_s[...] + pv

        m_old = md_s[...]
        m_new = jnp.maximum(m_old, jnp.max(s_d, axis=1, keepdims=True))
        alpha = jnp.exp2(m_old - m_new)
        pr = jnp.exp2(s_d - m_new)
        ld_s[...] = alpha * ld_s[...] + jnp.sum(pr, axis=1, keepdims=True)
        md_s[...] = m_new
        pb = pr.astype(BF16)
        for h in range(DIFF_HEADS):
            rows = slice(h * 16, (h + 1) * 16)
            vh = cat([pg_[4][pl.ds(h, 128, stride=4), :].astype(BF16) for pg_ in pages], 0)
            pv = jnp.dot(pb[rows, :], vh, preferred_element_type=F32)
            accd_s[rows, :] = alpha[rows, :] * accd_s[rows, :] + pv

    attend([(fk_p[i][0], fv_p[i][0], lf_p[i][0], dk_p[i][0], dv_p[i].at[0]) for i in range(pg)], None)

    @pl.when(g == ng - 1)
    def _():
        nk_s[...] = jnp.zeros(nk_s.shape, F32)
        nv_s[...] = jnp.zeros(nv_s.shape, F32)
        nl_s[...] = jnp.zeros(nl_s.shape, F32)
        ndk_s[...] = jnp.zeros(ndk_s.shape, F32)
        ndv_s[...] = jnp.zeros(ndv_s.shape, F32)
        nk_s[:, 0:8] = kfn_ref[0]
        nv_s[:, 0:8] = vfn_ref[0]
        nl_s[:, 0:8] = lfn_ref[0]
        ndk_s[:, 0:8] = kdn_ref[0]
        ndv_s[0:32, :] = vdn_ref[0]
        r = lax.broadcasted_iota(jnp.int32, (64, 128), 0)
        c = lax.broadcasted_iota(jnp.int32, (64, 128), 1)
        attend([(nk_s[...], nv_s[...], nl_s[...], ndk_s[...], ndv_s)], c <= (r & 7))

        of = jnp.where(blockdiag, accf_s[...] / lf_s[...], 0.0)
        of = jnp.sum(of.reshape(8, 8, 512), axis=0)
        ogf = of * gzf_ref[0]
        lam = _diff_lambda(lq1, lk1, lq2, lk2)
        on = accd_s[...] / ld_s[...]
        outs = [ogf]
        for h in range(DIFF_HEADS):
            od = on[h * 16:h * 16 + 8, :] - lam * on[h * 16 + 8:h * 16 + 16, :]
            ms = jnp.mean(od * od, axis=-1, keepdims=True)
            y = (od * lax.rsqrt(ms + NORM_EPS)) * gs_ref[...] * (1.0 - LAMBDA_INIT)
            outs.append(y * gzd_ref[0][:, h * 128:(h + 1) * 128])
        og_ref[0] = jnp.concatenate(outs, axis=1).astype(BF16)


def _out_t_body(x_ref, of_ref, od_ref, woT_ref, o_ref):
    y_t = jnp.dot(woT_ref[:, 0:512], of_ref[...], preferred_element_type=F32)
    y_t = y_t + jnp.dot(woT_ref[:, 512:1024], od_ref[...], preferred_element_type=F32)
    o_ref[...] = x_ref[...] + y_t.T


def _out_proj_t(x2d, ogf_t, ogd_t, wo_t, tm):
    n_tok = x2d.shape[0]
    return pl.pallas_call(
        _out_t_body,
        grid=(n_tok // tm,),
        in_specs=[pl.BlockSpec((tm, D_MODEL), lambda i: (i, 0)),
                  pl.BlockSpec((512, tm), lambda i: (0, i)),
                  pl.BlockSpec((512, tm), lambda i: (0, i)),
                  pl.BlockSpec((D_MODEL, D_MODEL), lambda i: (0, 0))],
        out_specs=pl.BlockSpec((tm, D_MODEL), lambda i: (i, 0)),
        out_shape=jax.ShapeDtypeStruct((n_tok, D_MODEL), F32),
        compiler_params=pltpu.CompilerParams(dimension_semantics=("arbitrary",),
                                             vmem_limit_bytes=VMEM_LIMIT),
        name="out_proj_prompt",
    )(x2d, ogf_t, ogd_t, wo_t)


def _out_body(x_ref, og_ref, wo_ref, o_ref):
    o_ref[...] = x_ref[...] + jnp.dot(og_ref[...], wo_ref[...], preferred_element_type=F32)


def _out_proj(x2d, og, wo, tm):
    n_tok = x2d.shape[0]
    return pl.pallas_call(
        _out_body,
        grid=(n_tok // tm,),
        in_specs=[pl.BlockSpec((tm, D_MODEL), lambda i: (i, 0)),
                  pl.BlockSpec((tm, D_MODEL), lambda i: (i, 0)),
                  pl.BlockSpec((D_MODEL, D_MODEL), lambda i: (0, 0))],
        out_specs=pl.BlockSpec((tm, D_MODEL), lambda i: (i, 0)),
        out_shape=jax.ShapeDtypeStruct((n_tok, D_MODEL), F32),
        compiler_params=pltpu.CompilerParams(dimension_semantics=("arbitrary",),
                                             vmem_limit_bytes=VMEM_LIMIT),
        name="out_proj_sample",
    )(x2d, og, wo)


def kernel(x_prompt, x_sample, cache_fox_k, cache_fox_v, cache_fox_logf, cache_diff_k, cache_diff_v,
           page_table, g_norm, w_in, b_f, g_fox_q, g_fox_k, g_diff_q, g_diff_k,
           lam_q1, lam_k1, lam_q2, lam_k2, g_subln, w_out):
    seq = x_prompt.shape[1]
    n_seq, dec = x_sample.shape[0], x_sample.shape[1]
    past = page_table.shape[1] * PAGE_SIZE
    n_pool = cache_fox_k.shape[1]

    w = w_in[0]
    sizes = [FOX_WIDTH, FOX_WIDTH, FOX_WIDTH, FOX_HEADS, FOX_WIDTH, DIFF_QK_WIDTH, DIFF_QK_WIDTH,
             DIFF_WIDTH, DIFF_WIDTH]
    offs = [0]
    for s in sizes:
        offs.append(offs[-1] + s)
    col = lambda k: w[:, offs[k]:offs[k + 1]]
    w_t = jnp.concatenate(
        [col(0), col(1), col(2), col(4), col(5), col(6), col(7), col(8), col(3),
         jnp.zeros((D_MODEL, 8), w.dtype)], axis=1).T.astype(BF16)
    w_dv = col(7).astype(BF16)
    half = ROT_DIM // 2
    inv = (ROPE_THETA ** (-jnp.arange(half, dtype=F32) * 2.0 / ROT_DIM)).reshape(half, 1)
    colv = lambda a: a[0].astype(F32).reshape(-1, 1)
    params = (g_norm.astype(F32), w_t, w_dv, colv(b_f), colv(g_fox_q), colv(g_fox_k),
              colv(g_diff_q), colv(g_diff_k), inv)
    lam_params = (lam_q1.astype(F32), lam_k1.astype(F32), lam_q2.astype(F32), lam_k2.astype(F32))
    wo = w_out[0].astype(BF16)

    xp2 = x_prompt[0]
    (fk_t, fv_t, lf_t, dk_t, dv_p, qfa_t, kfa, vfa_t, gzf_t, qd_t, kd, vda_t, gzd_t) = _project(
        xp2, params, True, 256, 0, 256, 1)

    xs_tb = jnp.transpose(x_sample, (1, 0, 2)).reshape(dec * n_seq, D_MODEL)
    (sfk, sfv, slf, sdk, sdv, sfq, sdq, sgzf, sgzd) = _project(
        xs_tb, params, False, n_seq, past, 1, 0)
    bt = lambda a: jnp.transpose(a, (2, 0, 1))
    bf = lambda a: jnp.transpose(a, (2, 1, 0))
    per_seq = (bt(sfq), bt(sdq), bf(sfk), bf(sfv), bf(slf), bf(sdk),
               jnp.transpose(sdv, (1, 0, 2)).reshape(n_seq, dec * DIFF_HEADS, DIFF_V_DIM),
               bt(sgzf), bt(sgzd))
    pools = (
        jnp.transpose(cache_fox_k[0], (0, 2, 3, 1)).reshape(n_pool, 512, PAGE_SIZE),
        jnp.transpose(cache_fox_v[0], (0, 2, 3, 1)).reshape(n_pool, 512, PAGE_SIZE),
        jnp.transpose(cache_fox_logf[0], (0, 2, 1)),
        jnp.transpose(cache_diff_k[0], (0, 2, 3, 4, 1)).reshape(n_pool, 512, PAGE_SIZE),
        cache_diff_v[0].reshape(n_pool, PAGE_SIZE * DIFF_HEADS, DIFF_V_DIM),
    )

    tq = tk = 1024
    n_a = (n_seq + 1) // 2
    sample_extra = lam_params + (g_subln.astype(F32),)
    ogf_t, og_sa = _fused_attention(False, qfa_t, kfa, vfa_t, gzf_t, g_fox_q, g_fox_k, (), seq, tq, tk,
                                    page_table, 0, n_a, pools, per_seq, sample_extra, SAMPLE_PAGES_PER_STEP)
    ogd_t, og_sb = _fused_attention(True, qd_t, kd, vda_t, gzd_t, g_diff_q, g_diff_k,
                                    lam_params + (colv(g_subln),), seq, tq, tk,
                                    page_table, n_a, n_seq - n_a, pools, per_seq, sample_extra,
                                    SAMPLE_PAGES_PER_STEP)
    og_s = jnp.concatenate([og_sa, og_sb], axis=0)
    y_prompt = _out_proj_t(xp2, ogf_t, ogd_t, wo.T, 256)[None]
    y_sample = _out_proj(x_sample.reshape(n_seq * dec, D_MODEL), og_s.reshape(n_seq * dec, D_MODEL),
                         wo, 256).reshape(n_seq, dec, D_MODEL)

    new_fk_p = jnp.transpose(fk_t.reshape(FOX_HEADS, HEAD_DIM, seq), (2, 0, 1))[None, None]
    new_fv_p = jnp.transpose(fv_t.reshape(FOX_HEADS, HEAD_DIM, seq), (2, 0, 1))[None, None]
    new_fl_p = jnp.transpose(lf_t, (1, 0))[None, None]
    new_dk_p = jnp.transpose(dk_t.reshape(DIFF_HEADS, 2, DIFF_QK_DIM, seq), (3, 0, 1, 2))[None, None]
    new_dv_p = dv_p.reshape(seq, DIFF_HEADS, DIFF_V_DIM)[None, None]
    new_fk_s = jnp.transpose(sfk.reshape(dec, FOX_HEADS, HEAD_DIM, n_seq), (3, 0, 1, 2))[None]
    new_fv_s = jnp.transpose(sfv.reshape(dec, FOX_HEADS, HEAD_DIM, n_seq), (3, 0, 1, 2))[None]
    new_fl_s = jnp.transpose(slf, (2, 0, 1))[None]
    new_dk_s = jnp.transpose(sdk.reshape(dec, DIFF_HEADS, 2, DIFF_QK_DIM, n_seq), (4, 0, 1, 2, 3))[None]
    new_dv_s = jnp.transpose(sdv, (1, 0, 2)).reshape(n_seq, dec, DIFF_HEADS, DIFF_V_DIM)[None]
    return (y_prompt, y_sample, new_fk_p, new_fv_p, new_fl_p, new_dk_p, new_dv_p,
            new_fk_s, new_fv_s, new_fl_s, new_dk_s, new_dv_s)
```

```python
import functools

import jax
import jax.numpy as jnp
from jax import lax
from jax.experimental import pallas as pl
from jax.experimental.pallas import tpu as pltpu

D_MODEL = 1024
HEAD_DIM = 64
FOX_HEADS = 8
FOX_WIDTH = FOX_HEADS * HEAD_DIM
DIFF_HEADS = 4
DIFF_QK_DIM = 64
DIFF_V_DIM = 2 * DIFF_QK_DIM
DIFF_QK_WIDTH = DIFF_HEADS * 2 * DIFF_QK_DIM
DIFF_WIDTH = DIFF_HEADS * DIFF_V_DIM
PAGE_SIZE = 128
ROPE_THETA = 500000.0
ROT_DIM = DIFF_QK_DIM // 4
NORM_EPS = 1e-6
LAMBDA_INIT = 0.2

LOG2E = 1.4426950408889634
QK_SCALE = LOG2E * HEAD_DIM ** -0.5
NEG = -1e30

F32 = jnp.float32
BF16 = jnp.bfloat16

AUG_DK = 128
SHIFT_LIMIT = 60.0
FOX_DVA = 80
DIFF_DVA = 144

VMEM_LIMIT = 56 * 1024 * 1024
SAMPLE_PAGES_PER_STEP = 8
PROMPT_COL_TILE = 512
PROMPT_ROW_TILE = 1024

R_FQ, R_FK, R_FV, R_FZ, R_DQ, R_DK, R_DV, R_DZ, R_FF = (i * 512 for i in range(9))
WT_ROWS = R_FF + 16


def _lane_cumsum(x):
    lane = lax.broadcasted_iota(jnp.int32, x.shape, 1)
    sh = 1
    while sh < x.shape[1]:
        x = x + jnp.where(lane >= sh, pltpu.roll(x, sh, 1), 0.0)
        sh *= 2
    return x


def _silu(z):
    return z * (1.0 / (1.0 + jnp.exp(-z)))


def _log_sigmoid(z):
    return jnp.minimum(z, 0.0) - jnp.log1p(jnp.exp(-jnp.abs(z)))


def _diff_lambda(lq1, lk1, lq2, lk2):
    a = jnp.sum(lq1[...] * lk1[...], axis=-1, keepdims=True)
    b = jnp.sum(lq2[...] * lk2[...], axis=-1, keepdims=True)
    return jnp.exp(a) - jnp.exp(b) + LAMBDA_INIT


def _proj_body(is_prompt, tm, pos_base, pos_tile, pos_lane, *refs):
    (x_ref, gn_ref, wT_ref, wdv_ref, bf_ref, gfq_ref, gfk_ref, gdq_ref, gdk_ref, inv_ref) = refs[:10]
    rest = refs[10:]
    i = pl.program_id(0)

    x = x_ref[...]
    ms = jnp.mean(x * x, axis=-1, keepdims=True)
    xn = (x * lax.rsqrt(ms + NORM_EPS)) * gn_ref[...]
    xb = xn.astype(BF16)

    def proj_t(r0, n):
        return lax.dot_general(wT_ref[r0:r0 + n, :], xb, (((1,), (1,)), ((), ())),
                               preferred_element_type=F32)

    def head_norm(h_t, g_ref):
        y = h_t.reshape(8, 64, tm)
        m = jnp.mean(y * y, axis=1, keepdims=True)
        return (y * lax.rsqrt(m + NORM_EPS)) * g_ref[...][None]

    pos = pos_base + i * pos_tile + lax.broadcasted_iota(jnp.int32, (1, tm), 1) * pos_lane
    ang = pos.astype(F32) * inv_ref[...]
    cos = jnp.cos(ang)[None]
    sin = jnp.sin(ang)[None]

    def rope(y):
        x1 = y[:, 0:8, :]
        x2 = y[:, 8:16, :]
        return jnp.concatenate([x1 * cos - x2 * sin, x2 * cos + x1 * sin, y[:, 16:, :]], axis=1)

    fq = head_norm(proj_t(R_FQ, 512), gfq_ref)
    fk = head_norm(proj_t(R_FK, 512), gfk_ref)
    fv = proj_t(R_FV, 512)
    lf = _log_sigmoid(proj_t(R_FF, 16)[0:8, :] + bf_ref[...])
    gzf = _silu(proj_t(R_FZ, 512))
    dq = rope(head_norm(proj_t(R_DQ, 512), gdq_ref))
    dk = rope(head_norm(proj_t(R_DK, 512), gdk_ref))
    dv_t = proj_t(R_DV, 512)
    gzd = _silu(proj_t(R_DZ, 512))
    dv_rm = jnp.dot(xb, wdv_ref[...], preferred_element_type=F32)

    fk2 = fk.reshape(512, tm)
    dk2 = dk.reshape(512, tm)
    fq2 = fq.reshape(512, tm) * QK_SCALE
    dq2 = dq.reshape(512, tm) * QK_SCALE

    if not is_prompt:
        (fk_o, fv_o, lf_o, dk_o, dv_o, fq_o, dq_o, gzf_o, gzd_o) = rest
        fk_o[...] = fk2
        fv_o[...] = fv
        lf_o[...] = lf
        dk_o[...] = dk2
        dv_o[...] = dv_rm
        fq_o[...] = fq2
        dq_o[...] = dq2
        gzf_o[...] = gzf
        gzd_o[...] = gzd
        return

    (fk_o, fv_o, lf_o, dk_o, dv_o, qfa_o, kfa_o, vfa_o, gzf_o, qd_o, kd_o, vda_o, gzd_o, carry) = rest
    fk_o[...] = fk2
    fv_o[...] = fv
    lf_o[...] = lf
    dk_o[...] = dk2
    for h in range(DIFF_HEADS):
        dv_o[pl.ds(h, tm, stride=DIFF_HEADS), :] = dv_rm[:, h * DIFF_V_DIM:(h + 1) * DIFF_V_DIM]
    gzf_o[...] = gzf.astype(BF16)
    gzd_o[...] = gzd.astype(BF16)

    @pl.when(i == 0)
    def _():
        carry[...] = jnp.zeros_like(carry)

    run = carry[:, 0:1]
    chunks = []
    for j in range(tm // 128):
        cj = run + _lane_cumsum(lf[:, j * 128:(j + 1) * 128])
        run = cj[:, 127:128]
        chunks.append(cj)
    carry[...] = jnp.broadcast_to(run, carry.shape)
    c2 = jnp.concatenate(chunks, axis=1) * LOG2E

    def split3(v):
        hi = v.astype(BF16).astype(F32)
        r1 = v - hi
        mid = r1.astype(BF16).astype(F32)
        return hi, mid, (r1 - mid).astype(BF16).astype(F32)

    rowid = lax.broadcasted_iota(jnp.int32, (8, tm), 0)

    def slab(pieces, h):
        return jnp.where(rowid == 0, pieces[0][h:h + 1, :],
                         jnp.where(rowid == 1, pieces[1][h:h + 1, :],
                                   jnp.where(rowid == 2, pieces[2][h:h + 1, :], 0.0)))

    def shift_bound(q2, g_ref):
        n = jnp.sqrt(jnp.sum(jnp.square(q2.reshape(8, 64, tm)), axis=1))
        return n * (8.0 * jnp.max(jnp.abs(g_ref[...]), axis=0, keepdims=True))

    c_pieces = split3(c2)
    mf_pieces = split3(shift_bound(fq2, gfk_ref))
    md_pieces = split3(shift_bound(dq2, gdk_ref))
    ones3 = jnp.where(rowid < 3, 1.0, 0.0)
    zeros8 = jnp.zeros((8, tm), F32)
    zeros40 = jnp.zeros((40, tm), F32)
    one_row16 = jnp.where(lax.broadcasted_iota(jnp.int32, (16, tm), 0) == 0, 1.0, 0.0)

    for h in range(FOX_HEADS):
        cs = slab(c_pieces, h)
        q_aug = jnp.concatenate([fq2[h * 64:(h + 1) * 64, :], cs, ones3, -slab(mf_pieces, h), zeros40], axis=0)
        k_aug = jnp.concatenate([fk2[h * 64:(h + 1) * 64, :], ones3, -cs, ones3, zeros40], axis=0)
        qfa_o[h * AUG_DK:(h + 1) * AUG_DK, :] = q_aug.astype(BF16)
        kfa_o[:, h * AUG_DK:(h + 1) * AUG_DK] = k_aug.T.astype(BF16)
        v_aug = jnp.concatenate([fv[h * 64:(h + 1) * 64, :], one_row16], axis=0)
        vfa_o[h * FOX_DVA:(h + 1) * FOX_DVA, :] = v_aug.astype(BF16)

    for j in range(2 * DIFF_HEADS):
        q_aug = jnp.concatenate([dq2[j * 64:(j + 1) * 64, :], zeros8, zeros8, -slab(md_pieces, j), zeros40], axis=0)
        k_aug = jnp.concatenate([dk2[j * 64:(j + 1) * 64, :], zeros8, zeros8, ones3, zeros40], axis=0)
        qd_o[j * AUG_DK:(j + 1) * AUG_DK, :] = q_aug.astype(BF16)
        kd_o[:, j * AUG_DK:(j + 1) * AUG_DK] = k_aug.T.astype(BF16)

    for h in range(DIFF_HEADS):
        v_aug = jnp.concatenate([dv_t[h * 128:(h + 1) * 128, :], one_row16], axis=0)
        vda_o[h * DIFF_DVA:(h + 1) * DIFF_DVA, :] = v_aug.astype(BF16)


def _project(x2d, params, is_prompt, tm, pos_base, pos_tile, pos_lane):
    n_tok = x2d.shape[0]
    n_tiles = n_tok // tm
    full = lambda a: pl.BlockSpec(a.shape, lambda i: (0,) * a.ndim)
    in_specs = [pl.BlockSpec((tm, D_MODEL), lambda i: (i, 0))] + [full(p) for p in params]
    t_spec = lambda rows: pl.BlockSpec((rows, tm), lambda i: (0, i))
    r_spec = lambda cols: pl.BlockSpec((tm, cols), lambda i: (i, 0))
    if is_prompt:
        sds = jax.ShapeDtypeStruct
        out_shape = [
            sds((512, n_tok), F32), sds((512, n_tok), F32), sds((8, n_tok), F32), sds((512, n_tok), F32),
            sds((DIFF_HEADS * n_tok, DIFF_V_DIM), F32),
            sds((8 * AUG_DK, n_tok), BF16), sds((n_tok, 8 * AUG_DK), BF16),
            sds((FOX_HEADS * FOX_DVA, n_tok), BF16), sds((512, n_tok), BF16),
            sds((8 * AUG_DK, n_tok), BF16), sds((n_tok, 8 * AUG_DK), BF16),
            sds((DIFF_HEADS * DIFF_DVA, n_tok), BF16), sds((512, n_tok), BF16),
        ]
        out_specs = [
            t_spec(512), t_spec(512), t_spec(8), t_spec(512),
            pl.BlockSpec((DIFF_HEADS * tm, DIFF_V_DIM), lambda i: (i, 0)),
            t_spec(8 * AUG_DK), r_spec(8 * AUG_DK),
            t_spec(FOX_HEADS * FOX_DVA), t_spec(512),
            t_spec(8 * AUG_DK), r_spec(8 * AUG_DK), t_spec(DIFF_HEADS * DIFF_DVA), t_spec(512),
        ]
        scratch = [pltpu.VMEM((8, 128), F32)]
    else:
        sds = jax.ShapeDtypeStruct
        s3 = lambda rows: pl.BlockSpec((None, rows, tm), lambda i: (i, 0, 0))
        out_shape = [
            sds((n_tiles, 512, tm), F32), sds((n_tiles, 512, tm), F32), sds((n_tiles, 8, tm), F32),
            sds((n_tiles, 512, tm), F32), sds((n_tiles, tm, 512), F32),
            sds((n_tiles, 512, tm), F32), sds((n_tiles, 512, tm), F32),
            sds((n_tiles, 512, tm), F32), sds((n_tiles, 512, tm), F32),
        ]
        out_specs = [s3(512), s3(512), s3(8), s3(512),
                     pl.BlockSpec((None, tm, 512), lambda i: (i, 0, 0)),
                     s3(512), s3(512), s3(512), s3(512)]
        scratch = []
    return pl.pallas_call(
        functools.partial(_proj_body, is_prompt, tm, pos_base, pos_tile, pos_lane),
        grid=(n_tiles,),
        in_specs=in_specs,
        out_specs=out_specs,
        out_shape=out_shape,
        scratch_shapes=scratch,
        compiler_params=pltpu.CompilerParams(dimension_semantics=("arbitrary",),
                                             vmem_limit_bytes=VMEM_LIMIT),
        name="proj_prompt" if is_prompt else "proj_sample",
    )(x2d, *params)


def _attn_parts(is_diff, tq, tk, dva, qi, ki, in_refs, og_ref, m_s, acc_s):
    qT_ref, k_ref, v_ref, gz_ref = in_refs[:4]
    if is_diff:
        lq1, lk1, lq2, lk2, gs_ref = in_refs[4:]
    dk = AUG_DK

    def init():
        m_s[...] = jnp.full(m_s.shape, NEG, F32)
        acc_s[...] = jnp.zeros(acc_s.shape, F32)

    def update(masked, fixed):
        if fixed and not masked:
            for j in range(2):
                v0 = 0 if is_diff else j * dva
                for c in range(tq // PROMPT_COL_TILE):
                    cols = slice(c * PROMPT_COL_TILE, (c + 1) * PROMPT_COL_TILE)
                    for r in range(tk // PROMPT_ROW_TILE):
                        rows = slice(r * PROMPT_ROW_TILE, (r + 1) * PROMPT_ROW_TILE)
                        s = jnp.dot(k_ref[rows, j * dk:(j + 1) * dk], qT_ref[j * dk:(j + 1) * dk, cols],
                                    preferred_element_type=F32)
                        acc_s[j, :, cols] += jnp.dot(v_ref[v0:v0 + dva, rows], jnp.exp2(s).astype(BF16),
                                                     preferred_element_type=F32)
            return
        for j in range(2):
            s = jnp.dot(k_ref[:, j * dk:(j + 1) * dk], qT_ref[j * dk:(j + 1) * dk, :],
                        preferred_element_type=F32)
            if masked:
                kpos = ki * tk + lax.broadcasted_iota(jnp.int32, (tk, tq), 0)
                qpos = qi * tq + lax.broadcasted_iota(jnp.int32, (tk, tq), 1)
                s = jnp.where(kpos <= qpos, s, NEG)
            vj = v_ref[...] if is_diff else v_ref[j * dva:(j + 1) * dva, :]
            if fixed:
                acc_s[j] += jnp.dot(vj, jnp.exp2(s).astype(BF16), preferred_element_type=F32)
            else:
                m_old = m_s[j]
                m_new = jnp.maximum(m_old, jnp.max(s, axis=0, keepdims=True))
                alpha = jnp.exp2(m_old - m_new)
                pb = jnp.exp2(s - m_new).astype(BF16)
                acc_s[j] = alpha * acc_s[j] + jnp.dot(vj, pb, preferred_element_type=F32)
                m_s[j] = m_new

    def finalize():
        if is_diff:
            lam = _diff_lambda(lq1, lk1, lq2, lk2)
            a1 = acc_s[0]
            a2 = acc_s[1]
            od = a1[0:128, :] / a1[128:129, :] - lam * (a2[0:128, :] / a2[128:129, :])
            ms = jnp.mean(od * od, axis=0, keepdims=True)
            y = (od * lax.rsqrt(ms + NORM_EPS)) * gs_ref[...] * (1.0 - LAMBDA_INIT)
            og_ref[...] = (y * gz_ref[...].astype(F32)).astype(BF16)
        else:
            for j in range(2):
                a = acc_s[j]
                o = a[0:64, :] / a[64:65, :]
                og_ref[j * 64:(j + 1) * 64, :] = (o * gz_ref[j * 64:(j + 1) * 64, :].astype(F32)).astype(BF16)

    return init, update, finalize


def _causal_pairs(n_tok, tq, tk):
    qi_l, ki_l = [], []
    for qi in range(n_tok // tq):
        for ki in range(((qi + 1) * tq - 1) // tk + 1):
            qi_l.append(qi)
            ki_l.append(ki)
    return jnp.asarray(qi_l, jnp.int32), jnp.asarray(ki_l, jnp.int32)


def _fused_body(is_diff, tq, tk, dva, pg, ng, n_pairs, n_steps, n_sample_steps, n_prompt_in, *refs):
    qi_tab, ki_tab, fixed_tab = refs[:3]
    refs = refs[4:]
    n_sample_in = 5 * pg + N_SAMPLE_SEQ_INPUTS + 5
    prompt_in = refs[:n_prompt_in]
    sample_in = refs[n_prompt_in:n_prompt_in + n_sample_in]
    og_ref, ogs_ref = refs[n_prompt_in + n_sample_in:n_prompt_in + n_sample_in + 2]
    scratch = refs[n_prompt_in + n_sample_in + 2:]
    m_s, acc_s = scratch[:2]
    p = pl.program_id(1)
    step = pl.program_id(0) * n_steps + p
    pp = jnp.minimum(p, n_pairs - 1)
    qi = qi_tab[pp]
    ki = ki_tab[pp]
    g = lax.rem(step, ng)
    a_init, a_update, a_finalize = _attn_parts(is_diff, tq, tk, dva, qi, ki, prompt_in, og_ref, m_s, acc_s)
    s_clear, s_init, s_pages, s_final = _sample_parts(pg, sample_in, ogs_ref, scratch[2:])
    both = jnp.logical_and
    no = jnp.logical_not

    has_prompt = p < n_pairs
    has_sample = step < n_sample_steps
    fixed_shift = fixed_tab[0] == 1
    need_mask = (ki + 1) * tk - 1 > qi * tq
    pl.when(step == 0)(s_clear)
    pl.when(both(has_prompt, ki == 0))(a_init)
    pl.when(both(has_sample, g == 0))(s_init)

    usual = both(both(has_prompt, has_sample), both(fixed_shift, no(need_mask)))

    @pl.when(usual)
    def _():
        a_update(False, True)
        s_pages()

    @pl.when(no(usual))
    def _():
        for masked in (True, False):
            for fixed in (True, False):
                c_mask = need_mask if masked else no(need_mask)
                c_fixed = fixed_shift if fixed else no(fixed_shift)
                pl.when(both(has_prompt, both(c_mask, c_fixed)))(functools.partial(a_update, masked, fixed))
        pl.when(has_sample)(s_pages)

    pl.when(both(has_prompt, ki == ((qi + 1) * tq - 1) // tk))(a_finalize)
    pl.when(both(has_sample, g == ng - 1))(s_final)


def _fused_attention(is_diff, qT, k, vT, gzT, g_q, g_k, prompt_extra, n_tok, tq, tk,
                     page_table, seq0, n_seq_call, pools, per_seq, sample_extra, pg):
    dva = DIFF_DVA if is_diff else FOX_DVA
    v_rows = dva if is_diff else 2 * dva
    qi_tab, ki_tab = _causal_pairs(n_tok, tq, tk)
    n_pairs = qi_tab.shape[0]
    n_pages = page_table.shape[1]
    ng = n_pages // pg
    n_sample_steps = n_seq_call * ng
    n_steps = max(n_pairs, -(-n_sample_steps // 4))
    bound = QK_SCALE * HEAD_DIM * jnp.max(jnp.abs(g_q)) * jnp.max(jnp.abs(g_k))
    fixed_tab = (bound <= SHIFT_LIMIT).astype(jnp.int32).reshape(1)
    pt_flat = page_table.reshape(-1)

    def pmap(f):
        def index_map(h, p, qt, kt, fx, pt):
            pp = jnp.minimum(p, n_pairs - 1)
            return f(h, qt[pp], kt[pp])
        return index_map

    def sample_step(h, p):
        return jnp.minimum(h * n_steps + p, n_sample_steps - 1)

    def page_map(i):
        def index_map(h, p, qt, kt, fx, pt):
            return (pt[seq0 * n_pages + sample_step(h, p) * pg + i], 0, 0)
        return index_map

    def seq_map(h, p, qt, kt, fx, pt):
        return (seq0 + lax.div(sample_step(h, p), ng), 0, 0)

    def out_seq_map(h, p, qt, kt, fx, pt):
        return (lax.div(sample_step(h, p), ng), 0, 0)

    in_specs = [
        pl.BlockSpec((2 * AUG_DK, tq), pmap(lambda h, q, kk: (h, q))),
        pl.BlockSpec((tk, 2 * AUG_DK), pmap(lambda h, q, kk: (kk, h))),
        pl.BlockSpec((v_rows, tk), pmap(lambda h, q, kk: (h, kk))),
        pl.BlockSpec((128, tq), pmap(lambda h, q, kk: (h, q))),
    ] + [pl.BlockSpec(e.shape, pmap(lambda h, q, kk: (0, 0))) for e in prompt_extra]
    operands = [qT, k, vT, gzT, *prompt_extra]
    n_prompt_in = len(operands)
    for pool, shape in zip(pools, ((512, 128), (512, 128), (8, 128), (512, 128), (512, 128))):
        for i in range(pg):
            in_specs.append(pl.BlockSpec((1,) + shape, page_map(i)))
            operands.append(pool)
    assert len(per_seq) == N_SAMPLE_SEQ_INPUTS
    for a in per_seq:
        in_specs.append(pl.BlockSpec((1,) + a.shape[1:], seq_map))
        operands.append(a)
    for e in sample_extra:
        in_specs.append(pl.BlockSpec(e.shape, pmap(lambda h, q, kk: (0, 0))))
        operands.append(e)
    scratch = [
        pltpu.VMEM((2, 1, tq), F32), pltpu.VMEM((2, dva, tq), F32),
        pltpu.VMEM((64, 512), BF16), pltpu.VMEM((64, 512), BF16),
        pltpu.VMEM((64, 1), F32), pltpu.VMEM((64, 1), F32), pltpu.VMEM((64, 512), F32),
        pltpu.VMEM((64, 1), F32), pltpu.VMEM((64, 1), F32), pltpu.VMEM((64, 128), F32),
        pltpu.VMEM((8, 128), F32),
        pltpu.VMEM((512, 128), F32), pltpu.VMEM((512, 128), F32), pltpu.VMEM((8, 128), F32),
        pltpu.VMEM((512, 128), F32), pltpu.VMEM((512, 128), F32),
    ]
    assert len(scratch) == 2 + N_SAMPLE_SCRATCH
    grid_spec = pltpu.PrefetchScalarGridSpec(
        num_scalar_prefetch=4,
        grid=(4, n_steps),
        in_specs=in_specs,
        out_specs=[pl.BlockSpec((128, tq), pmap(lambda h, q, kk: (h, q))),
                   pl.BlockSpec((1, 8, 1024), out_seq_map)],
        scratch_shapes=scratch,
    )
    return pl.pallas_call(
        functools.partial(_fused_body, is_diff, tq, tk, dva, pg, ng, n_pairs, n_steps,
                          n_sample_steps, n_prompt_in),
        grid_spec=grid_spec,
        out_shape=[jax.ShapeDtypeStruct((512, n_tok), BF16),
                   jax.ShapeDtypeStruct((n_seq_call, 8, 1024), BF16)],
        compiler_params=pltpu.CompilerParams(dimension_semantics=("arbitrary", "arbitrary"),
                                             vmem_limit_bytes=VMEM_LIMIT),
        name="attn_diff" if is_diff else "attn_fox",
    )(qi_tab, ki_tab, fixed_tab, pt_flat, *operands)


N_SAMPLE_SEQ_INPUTS = 9
N_SAMPLE_SCRATCH = 14


def _sample_parts(pg, in_refs, og_ref, scratch):
    fk_p = in_refs[0 * pg:1 * pg]
    fv_p = in_refs[1 * pg:2 * pg]
    lf_p = in_refs[2 * pg:3 * pg]
    dk_p = in_refs[3 * pg:4 * pg]
    dv_p = in_refs[4 * pg:5 * pg]
    (qf_ref, qd_ref, kfn_ref, vfn_ref, lfn_ref, kdn_ref, vdn_ref, gzf_ref, gzd_ref,
     lq1, lk1, lq2, lk2, gs_ref) = in_refs[5 * pg:]
    (qbf_s, qbd_s, mf_s, lf_s, accf_s, md_s, ld_s, accd_s, off_s,
     nk_s, nv_s, nl_s, ndk_s, ndv_s) = scratch

    def blockdiag():
        row = lax.broadcasted_iota(jnp.int32, (64, 512), 0)
        col = lax.broadcasted_iota(jnp.int32, (64, 512), 1)
        return (row >> 3) == (col >> 6)

    def init():
        qf = jnp.concatenate([qf_ref[0]] * 8, axis=0)
        qd = jnp.concatenate([qd_ref[0]] * 8, axis=0)
        qbf_s[...] = jnp.where(blockdiag(), qf, 0.0).astype(BF16)
        qbd_s[...] = jnp.where(blockdiag(), qd, 0.0).astype(BF16)
        mf_s[...] = jnp.full(mf_s.shape, NEG, F32)
        md_s[...] = jnp.full(md_s.shape, NEG, F32)
        lf_s[...] = jnp.zeros(lf_s.shape, F32)
        ld_s[...] = jnp.zeros(ld_s.shape, F32)
        accf_s[...] = jnp.zeros(accf_s.shape, F32)
        accd_s[...] = jnp.zeros(accd_s.shape, F32)
        off_s[...] = jnp.zeros(off_s.shape, F32)

    def attend(pages, mask):
        n = len(pages)
        cat = lambda xs, axis: xs[0] if n == 1 else jnp.concatenate(xs, axis=axis)
        cw = _lane_cumsum(cat([pg_[2] for pg_ in pages], 0))
        off = off_s[:, 0:1]
        cs = []
        for i in range(n):
            cwi = cw[i * 8:(i + 1) * 8, :]
            c = (off + cwi) * LOG2E
            cs.append(jnp.broadcast_to(c[:, None, :], (8, 8, 128)).reshape(64, 128))
            off = off + cwi[:, 127:128]
        off_s[...] = jnp.broadcast_to(off, off_s.shape)

        kf = cat([pg_[0].astype(BF16) for pg_ in pages], 1)
        s_f = jnp.dot(qbf_s[...], kf, preferred_element_type=F32) - cat(cs, 1)
        kd = cat([pg_[3].astype(BF16) for pg_ in pages], 1)
        s_d = jnp.dot(qbd_s[...], kd, preferred_element_type=F32)
        if mask is not None:
            s_f = jnp.where(mask, s_f, NEG)
            s_d = jnp.where(mask, s_d, NEG)

        m_old = mf_s[...]
        m_new = jnp.maximum(m_old, jnp.max(s_f, axis=1, keepdims=True))
        alpha = jnp.exp2(m_old - m_new)
        pr = jnp.exp2(s_f - m_new)
        lf_s[...] = alpha * lf_s[...] + jnp.sum(pr, axis=1, keepdims=True)
        mf_s[...] = m_new
        vf = cat([pg_[1].astype(BF16) for pg_ in pages], 1)
        pv = lax.dot_general(pr.astype(BF16), vf, (((1,), (1,)), ((), ())), preferred_element_type=F32)
        accf_s[...] = alpha * accf_s[...] + pv

        m_old = md_s[...]
        m_new = jnp.maximum(m_old, jnp.max(s_d, axis=1, keepdims=True))
        alpha = jnp.exp2(m_old - m_new)
        pr = jnp.exp2(s_d - m_new)
        ld_s[...] = alpha * ld_s[...] + jnp.sum(pr, axis=1, keepdims=True)
        md_s[...] = m_new
        pb = pr.astype(BF16)
        for h in range(DIFF_HEADS):
            rows = slice(h * 16, (h + 1) * 16)
            vh = cat([pg_[4][pl.ds(h, 128, stride=4), :].astype(BF16) for pg_ in pages], 0)
            pv = jnp.dot(pb[rows, :], vh, preferred_element_type=F32)
            accd_s[rows, :] = alpha[rows, :] * accd_s[rows, :] + pv

    def pages():
        attend([(fk_p[i][0], fv_p[i][0], lf_p[i][0], dk_p[i][0], dv_p[i].at[0]) for i in range(pg)], None)

    def clear_new_page():
        nk_s[...] = jnp.zeros(nk_s.shape, F32)
        nv_s[...] = jnp.zeros(nv_s.shape, F32)
        nl_s[...] = jnp.zeros(nl_s.shape, F32)
        ndk_s[...] = jnp.zeros(ndk_s.shape, F32)
        ndv_s[...] = jnp.zeros(ndv_s.shape, F32)

    def final():
        nk_s[:, 0:8] = kfn_ref[0]
        nv_s[:, 0:8] = vfn_ref[0]
        nl_s[:, 0:8] = lfn_ref[0]
        ndk_s[:, 0:8] = kdn_ref[0]
        ndv_s[0:32, :] = vdn_ref[0]
        r = lax.broadcasted_iota(jnp.int32, (64, 128), 0)
        c = lax.broadcasted_iota(jnp.int32, (64, 128), 1)
        attend([(nk_s[...], nv_s[...], nl_s[...], ndk_s[...], ndv_s)], c <= (r & 7))

        of = jnp.where(blockdiag(), accf_s[...] / lf_s[...], 0.0)
        of = jnp.sum(of.reshape(8, 8, 512), axis=0)
        ogf = of * gzf_ref[0]
        lam = _diff_lambda(lq1, lk1, lq2, lk2)
        on = accd_s[...] / ld_s[...]
        outs = [ogf]
        for h in range(DIFF_HEADS):
            od = on[h * 16:h * 16 + 8, :] - lam * on[h * 16 + 8:h * 16 + 16, :]
            ms = jnp.mean(od * od, axis=-1, keepdims=True)
            y = (od * lax.rsqrt(ms + NORM_EPS)) * gs_ref[...] * (1.0 - LAMBDA_INIT)
            outs.append(y * gzd_ref[0][:, h * 128:(h + 1) * 128])
        og_ref[0] = jnp.concatenate(outs, axis=1).astype(BF16)

    return clear_new_page, init, pages, final


def _out_t_body(x_ref, of_ref, od_ref, woT_ref, o_ref):
    y_t = jnp.dot(woT_ref[:, 0:512], of_ref[...], preferred_element_type=F32)
    y_t = y_t + jnp.dot(woT_ref[:, 512:1024], od_ref[...], preferred_element_type=F32)
    o_ref[...] = x_ref[...] + y_t.T


def _out_proj_t(x2d, ogf_t, ogd_t, wo_t, tm):
    n_tok = x2d.shape[0]
    return pl.pallas_call(
        _out_t_body,
        grid=(n_tok // tm,),
        in_specs=[pl.BlockSpec((tm, D_MODEL), lambda i: (i, 0)),
                  pl.BlockSpec((512, tm), lambda i: (0, i)),
                  pl.BlockSpec((512, tm), lambda i: (0, i)),
                  pl.BlockSpec((D_MODEL, D_MODEL), lambda i: (0, 0))],
        out_specs=pl.BlockSpec((tm, D_MODEL), lambda i: (i, 0)),
        out_shape=jax.ShapeDtypeStruct((n_tok, D_MODEL), F32),
        compiler_params=pltpu.CompilerParams(dimension_semantics=("arbitrary",),
                                             vmem_limit_bytes=VMEM_LIMIT),
        name="out_proj_prompt",
    )(x2d, ogf_t, ogd_t, wo_t)


def _out_body(x_ref, og_ref, wo_ref, o_ref):
    o_ref[...] = x_ref[...] + jnp.dot(og_ref[...], wo_ref[...], preferred_element_type=F32)


def _out_proj(x2d, og, wo, tm):
    n_tok = x2d.shape[0]
    return pl.pallas_call(
        _out_body,
        grid=(n_tok // tm,),
        in_specs=[pl.BlockSpec((tm, D_MODEL), lambda i: (i, 0)),
                  pl.BlockSpec((tm, D_MODEL), lambda i: (i, 0)),
                  pl.BlockSpec((D_MODEL, D_MODEL), lambda i: (0, 0))],
        out_specs=pl.BlockSpec((tm, D_MODEL), lambda i: (i, 0)),
        out_shape=jax.ShapeDtypeStruct((n_tok, D_MODEL), F32),
        compiler_params=pltpu.CompilerParams(dimension_semantics=("arbitrary",),
                                             vmem_limit_bytes=VMEM_LIMIT),
        name="out_proj_sample",
    )(x2d, og, wo)


def kernel(x_prompt, x_sample, cache_fox_k, cache_fox_v, cache_fox_logf, cache_diff_k, cache_diff_v,
           page_table, g_norm, w_in, b_f, g_fox_q, g_fox_k, g_diff_q, g_diff_k,
           lam_q1, lam_k1, lam_q2, lam_k2, g_subln, w_out):
    seq = x_prompt.shape[1]
    n_seq, dec = x_sample.shape[0], x_sample.shape[1]
    past = page_table.shape[1] * PAGE_SIZE
    n_pool = cache_fox_k.shape[1]

    w = w_in[0]
    sizes = [FOX_WIDTH, FOX_WIDTH, FOX_WIDTH, FOX_HEADS, FOX_WIDTH, DIFF_QK_WIDTH, DIFF_QK_WIDTH,
             DIFF_WIDTH, DIFF_WIDTH]
    offs = [0]
    for s in sizes:
        offs.append(offs[-1] + s)
    col = lambda k: w[:, offs[k]:offs[k + 1]]
    w_t = jnp.concatenate(
        [col(0), col(1), col(2), col(4), col(5), col(6), col(7), col(8), col(3),
         jnp.zeros((D_MODEL, 8), w.dtype)], axis=1).T.astype(BF16)
    w_dv = col(7).astype(BF16)
    half = ROT_DIM // 2
    inv = (ROPE_THETA ** (-jnp.arange(half, dtype=F32) * 2.0 / ROT_DIM)).reshape(half, 1)
    colv = lambda a: a[0].astype(F32).reshape(-1, 1)
    params = (g_norm.astype(F32), w_t, w_dv, colv(b_f), colv(g_fox_q), colv(g_fox_k),
              colv(g_diff_q), colv(g_diff_k), inv)
    lam_params = (lam_q1.astype(F32), lam_k1.astype(F32), lam_q2.astype(F32), lam_k2.astype(F32))
    wo = w_out[0].astype(BF16)

    xp2 = x_prompt[0]
    (fk_t, fv_t, lf_t, dk_t, dv_p, qfa_t, kfa, vfa_t, gzf_t, qd_t, kd, vda_t, gzd_t) = _project(
        xp2, params, True, 256, 0, 256, 1)

    xs_tb = jnp.transpose(x_sample, (1, 0, 2)).reshape(dec * n_seq, D_MODEL)
    (sfk, sfv, slf, sdk, sdv, sfq, sdq, sgzf, sgzd) = _project(
        xs_tb, params, False, n_seq, past, 1, 0)
    bt = lambda a: jnp.transpose(a, (2, 0, 1))
    bf = lambda a: jnp.transpose(a, (2, 1, 0))
    per_seq = (bt(sfq), bt(sdq), bf(sfk), bf(sfv), bf(slf), bf(sdk),
               jnp.transpose(sdv, (1, 0, 2)).reshape(n_seq, dec * DIFF_HEADS, DIFF_V_DIM),
               bt(sgzf), bt(sgzd))
    pools = (
        jnp.transpose(cache_fox_k[0], (0, 2, 3, 1)).reshape(n_pool, 512, PAGE_SIZE),
        jnp.transpose(cache_fox_v[0], (0, 2, 3, 1)).reshape(n_pool, 512, PAGE_SIZE),
        jnp.transpose(cache_fox_logf[0], (0, 2, 1)),
        jnp.transpose(cache_diff_k[0], (0, 2, 3, 4, 1)).reshape(n_pool, 512, PAGE_SIZE),
        cache_diff_v[0].reshape(n_pool, PAGE_SIZE * DIFF_HEADS, DIFF_V_DIM),
    )

    tq = tk = 1024
    n_a = (n_seq + 1) // 2
    sample_extra = lam_params + (g_subln.astype(F32),)
    ogf_t, og_sa = _fused_attention(False, qfa_t, kfa, vfa_t, gzf_t, g_fox_q, g_fox_k, (), seq, tq, tk,
                                    page_table, 0, n_a, pools, per_seq, sample_extra, SAMPLE_PAGES_PER_STEP)
    ogd_t, og_sb = _fused_attention(True, qd_t, kd, vda_t, gzd_t, g_diff_q, g_diff_k,
                                    lam_params + (colv(g_subln),), seq, tq, tk,
                                    page_table, n_a, n_seq - n_a, pools, per_seq, sample_extra,
                                    SAMPLE_PAGES_PER_STEP)
    og_s = jnp.concatenate([og_sa, og_sb], axis=0)
    y_prompt = _out_proj_t(xp2, ogf_t, ogd_t, wo.T, 256)[None]
    y_sample = _out_proj(x_sample.reshape(n_seq * dec, D_MODEL), og_s.reshape(n_seq * dec, D_MODEL),
                         wo, 256).reshape(n_seq, dec, D_MODEL)

    new_fk_p = jnp.transpose(fk_t.reshape(FOX_HEADS, HEAD_DIM, seq), (2, 0, 1))[None, None]
    new_fv_p = jnp.transpose(fv_t.reshape(FOX_HEADS, HEAD_DIM, seq), (2, 0, 1))[None, None]
    new_fl_p = jnp.transpose(lf_t, (1, 0))[None, None]
    new_dk_p = jnp.transpose(dk_t.reshape(DIFF_HEADS, 2, DIFF_QK_DIM, seq), (3, 0, 1, 2))[None, None]
    new_dv_p = dv_p.reshape(seq, DIFF_HEADS, DIFF_V_DIM)[None, None]
    new_fk_s = jnp.transpose(sfk.reshape(dec, FOX_HEADS, HEAD_DIM, n_seq), (3, 0, 1, 2))[None]
    new_fv_s = jnp.transpose(sfv.reshape(dec, FOX_HEADS, HEAD_DIM, n_seq), (3, 0, 1, 2))[None]
    new_fl_s = jnp.transpose(slf, (2, 0, 1))[None]
    new_dk_s = jnp.transpose(sdk.reshape(dec, DIFF_HEADS, 2, DIFF_QK_DIM, n_seq), (4, 0, 1, 2, 3))[None]
    new_dv_s = jnp.transpose(sdv, (1, 0, 2)).reshape(n_seq, dec, DIFF_HEADS, DIFF_V_DIM)[None]
    return (y_prompt, y_sample, new_fk_p, new_fv_p, new_fl_p, new_dk_p, new_dv_p,
            new_fk_s, new_fv_s, new_fl_s, new_dk_s, new_dv_s)
```

```python
import functools

import jax
import jax.numpy as jnp
from jax import lax
from jax.experimental import pallas as pl
from jax.experimental.pallas import tpu as pltpu

D_MODEL = 1024
HEAD_DIM = 64
FOX_HEADS = 8
FOX_WIDTH = FOX_HEADS * HEAD_DIM
DIFF_HEADS = 4
DIFF_QK_DIM = 64
DIFF_V_DIM = 2 * DIFF_QK_DIM
DIFF_QK_WIDTH = DIFF_HEADS * 2 * DIFF_QK_DIM
DIFF_WIDTH = DIFF_HEADS * DIFF_V_DIM
PAGE_SIZE = 128
ROPE_THETA = 500000.0
ROT_DIM = DIFF_QK_DIM // 4
NORM_EPS = 1e-6
LAMBDA_INIT = 0.2

LOG2E = 1.4426950408889634
QK_SCALE = LOG2E * HEAD_DIM ** -0.5
NEG = -1e30

F32 = jnp.float32
BF16 = jnp.bfloat16

AUG_DK = 128
SHIFT_LIMIT = 60.0
FOX_DVA = 80
DIFF_DVA = 144

VMEM_LIMIT = 56 * 1024 * 1024
SAMPLE_PAGES_PER_STEP = 8
PROMPT_COL_TILE = 512
PROMPT_ROW_TILE = 1024

R_FQ, R_FK, R_FV, R_FZ, R_DQ, R_DK, R_DV, R_DZ, R_FF = (i * 512 for i in range(9))
WT_ROWS = R_FF + 16


def _lane_cumsum(x):
    lane = lax.broadcasted_iota(jnp.int32, x.shape, 1)
    sh = 1
    while sh < x.shape[1]:
        x = x + jnp.where(lane >= sh, pltpu.roll(x, sh, 1), 0.0)
        sh *= 2
    return x


def _silu(z):
    return z * (1.0 / (1.0 + jnp.exp(-z)))


def _log_sigmoid(z):
    return jnp.minimum(z, 0.0) - jnp.log1p(jnp.exp(-jnp.abs(z)))


def _diff_lambda(lq1, lk1, lq2, lk2):
    a = jnp.sum(lq1[...] * lk1[...], axis=-1, keepdims=True)
    b = jnp.sum(lq2[...] * lk2[...], axis=-1, keepdims=True)
    return jnp.exp(a) - jnp.exp(b) + LAMBDA_INIT


def _proj_body(is_prompt, tm, pos_base, pos_tile, pos_lane, *refs):
    (x_ref, gn_ref, wT_ref, wdv_ref, bf_ref, gfq_ref, gfk_ref, gdq_ref, gdk_ref, inv_ref) = refs[:10]
    rest = refs[10:]
    i = pl.program_id(0)

    x = x_ref[...]
    ms = jnp.mean(x * x, axis=-1, keepdims=True)
    xn = (x * lax.rsqrt(ms + NORM_EPS)) * gn_ref[...]
    xb = xn.astype(BF16)

    def proj_t(r0, n):
        return lax.dot_general(wT_ref[r0:r0 + n, :], xb, (((1,), (1,)), ((), ())),
                               preferred_element_type=F32)

    def head_norm(h_t, g_ref):
        y = h_t.reshape(8, 64, tm)
        m = jnp.mean(y * y, axis=1, keepdims=True)
        return (y * lax.rsqrt(m + NORM_EPS)) * g_ref[...][None]

    pos = pos_base + i * pos_tile + lax.broadcasted_iota(jnp.int32, (1, tm), 1) * pos_lane
    ang = pos.astype(F32) * inv_ref[...]
    cos = jnp.cos(ang)[None]
    sin = jnp.sin(ang)[None]

    def rope(y):
        x1 = y[:, 0:8, :]
        x2 = y[:, 8:16, :]
        return jnp.concatenate([x1 * cos - x2 * sin, x2 * cos + x1 * sin, y[:, 16:, :]], axis=1)

    fq = head_norm(proj_t(R_FQ, 512), gfq_ref)
    fk = head_norm(proj_t(R_FK, 512), gfk_ref)
    fv = proj_t(R_FV, 512)
    lf = _log_sigmoid(proj_t(R_FF, 16)[0:8, :] + bf_ref[...])
    gzf = _silu(proj_t(R_FZ, 512))
    dq = rope(head_norm(proj_t(R_DQ, 512), gdq_ref))
    dk = rope(head_norm(proj_t(R_DK, 512), gdk_ref))
    dv_t = proj_t(R_DV, 512)
    gzd = _silu(proj_t(R_DZ, 512))
    dv_rm = jnp.dot(xb, wdv_ref[...], preferred_element_type=F32)

    fk2 = fk.reshape(512, tm)
    dk2 = dk.reshape(512, tm)
    fq2 = fq.reshape(512, tm) * QK_SCALE
    dq2 = dq.reshape(512, tm) * QK_SCALE

    if not is_prompt:
        (fk_o, fv_o, lf_o, dk_o, dv_o, fq_o, dq_o, gzf_o, gzd_o) = rest
        fk_o[...] = fk2
        fv_o[...] = fv
        lf_o[...] = lf
        dk_o[...] = dk2
        dv_o[...] = dv_rm
        fq_o[...] = fq2
        dq_o[...] = dq2
        gzf_o[...] = gzf
        gzd_o[...] = gzd
        return

    (fk_o, fv_o, lf_o, dk_o, dv_o, qfa_o, kfa_o, vfa_o, gzf_o, qd_o, kd_o, vda_o, gzd_o, carry) = rest
    fk_o[...] = fk2
    fv_o[...] = fv
    lf_o[...] = lf
    dk_o[...] = dk2
    for h in range(DIFF_HEADS):
        dv_o[pl.ds(h, tm, stride=DIFF_HEADS), :] = dv_rm[:, h * DIFF_V_DIM:(h + 1) * DIFF_V_DIM]
    gzf_o[...] = gzf.astype(BF16)
    gzd_o[...] = gzd.astype(BF16)

    @pl.when(i == 0)
    def _():
        carry[...] = jnp.zeros_like(carry)

    run = carry[:, 0:1]
    chunks = []
    for j in range(tm // 128):
        cj = run + _lane_cumsum(lf[:, j * 128:(j + 1) * 128])
        run = cj[:, 127:128]
        chunks.append(cj)
    carry[...] = jnp.broadcast_to(run, carry.shape)
    c2 = jnp.concatenate(chunks, axis=1) * LOG2E

    def split3(v):
        hi = v.astype(BF16).astype(F32)
        r1 = v - hi
        mid = r1.astype(BF16).astype(F32)
        return hi, mid, (r1 - mid).astype(BF16).astype(F32)

    rowid = lax.broadcasted_iota(jnp.int32, (8, tm), 0)

    def slab(pieces, h):
        return jnp.where(rowid == 0, pieces[0][h:h + 1, :],
                         jnp.where(rowid == 1, pieces[1][h:h + 1, :],
                                   jnp.where(rowid == 2, pieces[2][h:h + 1, :], 0.0)))

    def shift_bound(q2, g_ref):
        n = jnp.sqrt(jnp.sum(jnp.square(q2.reshape(8, 64, tm)), axis=1))
        return n * (8.0 * jnp.max(jnp.abs(g_ref[...]), axis=0, keepdims=True))

    c_pieces = split3(c2)
    mf_pieces = split3(shift_bound(fq2, gfk_ref))
    md_pieces = split3(shift_bound(dq2, gdk_ref))
    ones3 = jnp.where(rowid < 3, 1.0, 0.0)
    zeros8 = jnp.zeros((8, tm), F32)
    zeros40 = jnp.zeros((40, tm), F32)
    one_row16 = jnp.where(lax.broadcasted_iota(jnp.int32, (16, tm), 0) == 0, 1.0, 0.0)

    for h in range(FOX_HEADS):
        cs = slab(c_pieces, h)
        q_aug = jnp.concatenate([fq2[h * 64:(h + 1) * 64, :], cs, ones3, -slab(mf_pieces, h), zeros40], axis=0)
        k_aug = jnp.concatenate([fk2[h * 64:(h + 1) * 64, :], ones3, -cs, ones3, zeros40], axis=0)
        qfa_o[h * AUG_DK:(h + 1) * AUG_DK, :] = q_aug.astype(BF16)
        kfa_o[:, h * AUG_DK:(h + 1) * AUG_DK] = k_aug.T.astype(BF16)
        v_aug = jnp.concatenate([fv[h * 64:(h + 1) * 64, :], one_row16], axis=0)
        vfa_o[h * FOX_DVA:(h + 1) * FOX_DVA, :] = v_aug.astype(BF16)

    for j in range(2 * DIFF_HEADS):
        q_aug = jnp.concatenate([dq2[j * 64:(j + 1) * 64, :], zeros8, zeros8, -slab(md_pieces, j), zeros40], axis=0)
        k_aug = jnp.concatenate([dk2[j * 64:(j + 1) * 64, :], zeros8, zeros8, ones3, zeros40], axis=0)
        qd_o[j * AUG_DK:(j + 1) * AUG_DK, :] = q_aug.astype(BF16)
        kd_o[:, j * AUG_DK:(j + 1) * AUG_DK] = k_aug.T.astype(BF16)

    for h in range(DIFF_HEADS):
        v_aug = jnp.concatenate([dv_t[h * 128:(h + 1) * 128, :], one_row16], axis=0)
        vda_o[h * DIFF_DVA:(h + 1) * DIFF_DVA, :] = v_aug.astype(BF16)


def _project(x2d, params, is_prompt, tm, pos_base, pos_tile, pos_lane):
    n_tok = x2d.shape[0]
    n_tiles = n_tok // tm
    full = lambda a: pl.BlockSpec(a.shape, lambda i: (0,) * a.ndim)
    in_specs = [pl.BlockSpec((tm, D_MODEL), lambda i: (i, 0))] + [full(p) for p in params]
    t_spec = lambda rows: pl.BlockSpec((rows, tm), lambda i: (0, i))
    r_spec = lambda cols: pl.BlockSpec((tm, cols), lambda i: (i, 0))
    if is_prompt:
        sds = jax.ShapeDtypeStruct
        out_shape = [
            sds((512, n_tok), F32), sds((512, n_tok), F32), sds((8, n_tok), F32), sds((512, n_tok), F32),
            sds((DIFF_HEADS * n_tok, DIFF_V_DIM), F32),
            sds((8 * AUG_DK, n_tok), BF16), sds((n_tok, 8 * AUG_DK), BF16),
            sds((FOX_HEADS * FOX_DVA, n_tok), BF16), sds((512, n_tok), BF16),
            sds((8 * AUG_DK, n_tok), BF16), sds((n_tok, 8 * AUG_DK), BF16),
            sds((DIFF_HEADS * DIFF_DVA, n_tok), BF16), sds((512, n_tok), BF16),
        ]
        out_specs = [
            t_spec(512), t_spec(512), t_spec(8), t_spec(512),
            pl.BlockSpec((DIFF_HEADS * tm, DIFF_V_DIM), lambda i: (i, 0)),
            t_spec(8 * AUG_DK), r_spec(8 * AUG_DK),
            t_spec(FOX_HEADS * FOX_DVA), t_spec(512),
            t_spec(8 * AUG_DK), r_spec(8 * AUG_DK), t_spec(DIFF_HEADS * DIFF_DVA), t_spec(512),
        ]
        scratch = [pltpu.VMEM((8, 128), F32)]
    else:
        sds = jax.ShapeDtypeStruct
        s3 = lambda rows: pl.BlockSpec((None, rows, tm), lambda i: (i, 0, 0))
        out_shape = [
            sds((n_tiles, 512, tm), F32), sds((n_tiles, 512, tm), F32), sds((n_tiles, 8, tm), F32),
            sds((n_tiles, 512, tm), F32), sds((n_tiles, tm, 512), F32),
            sds((n_tiles, 512, tm), F32), sds((n_tiles, 512, tm), F32),
            sds((n_tiles, 512, tm), F32), sds((n_tiles, 512, tm), F32),
        ]
        out_specs = [s3(512), s3(512), s3(8), s3(512),
                     pl.BlockSpec((None, tm, 512), lambda i: (i, 0, 0)),
                     s3(512), s3(512), s3(512), s3(512)]
        scratch = []
    return pl.pallas_call(
        functools.partial(_proj_body, is_prompt, tm, pos_base, pos_tile, pos_lane),
        grid=(n_tiles,),
        in_specs=in_specs,
        out_specs=out_specs,
        out_shape=out_shape,
        scratch_shapes=scratch,
        compiler_params=pltpu.CompilerParams(dimension_semantics=("arbitrary",),
                                             vmem_limit_bytes=VMEM_LIMIT),
        name="proj_prompt" if is_prompt else "proj_sample",
    )(x2d, *params)


def _attn_parts(is_diff, tq, tk, dva, qi, ki, in_refs, og_ref, m_s, acc_s):
    qT_ref, k_ref, v_ref, gz_ref = in_refs[:4]
    if is_diff:
        lq1, lk1, lq2, lk2, gs_ref = in_refs[4:]
    dk = AUG_DK

    def init():
        m_s[...] = jnp.full(m_s.shape, NEG, F32)
        acc_s[...] = jnp.zeros(acc_s.shape, F32)

    def update(masked, fixed):
        if fixed and not masked:
            for j in range(2):
                v0 = 0 if is_diff else j * dva
                for c in range(tq // PROMPT_COL_TILE):
                    cols = slice(c * PROMPT_COL_TILE, (c + 1) * PROMPT_COL_TILE)
                    for r in range(tk // PROMPT_ROW_TILE):
                        rows = slice(r * PROMPT_ROW_TILE, (r + 1) * PROMPT_ROW_TILE)
                        s = jnp.dot(k_ref[rows, j * dk:(j + 1) * dk], qT_ref[j * dk:(j + 1) * dk, cols],
                                    preferred_element_type=F32)
                        acc_s[j, :, cols] += jnp.dot(v_ref[v0:v0 + dva, rows], jnp.exp2(s).astype(BF16),
                                                     preferred_element_type=F32)
            return
        for j in range(2):
            s = jnp.dot(k_ref[:, j * dk:(j + 1) * dk], qT_ref[j * dk:(j + 1) * dk, :],
                        preferred_element_type=F32)
            if masked:
                kpos = ki * tk + lax.broadcasted_iota(jnp.int32, (tk, tq), 0)
                qpos = qi * tq + lax.broadcasted_iota(jnp.int32, (tk, tq), 1)
                s = jnp.where(kpos <= qpos, s, NEG)
            vj = v_ref[...] if is_diff else v_ref[j * dva:(j + 1) * dva, :]
            if fixed:
                acc_s[j] += jnp.dot(vj, jnp.exp2(s).astype(BF16), preferred_element_type=F32)
            else:
                m_old = m_s[j]
                m_new = jnp.maximum(m_old, jnp.max(s, axis=0, keepdims=True))
                alpha = jnp.exp2(m_old - m_new)
                pb = jnp.exp2(s - m_new).astype(BF16)
                acc_s[j] = alpha * acc_s[j] + jnp.dot(vj, pb, preferred_element_type=F32)
                m_s[j] = m_new

    def finalize():
        if is_diff:
            lam = _diff_lambda(lq1, lk1, lq2, lk2)
            a1 = acc_s[0]
            a2 = acc_s[1]
            od = a1[0:128, :] / a1[128:129, :] - lam * (a2[0:128, :] / a2[128:129, :])
            ms = jnp.mean(od * od, axis=0, keepdims=True)
            y = (od * lax.rsqrt(ms + NORM_EPS)) * gs_ref[...] * (1.0 - LAMBDA_INIT)
            og_ref[...] = (y * gz_ref[...].astype(F32)).astype(BF16)
        else:
            for j in range(2):
                a = acc_s[j]
                o = a[0:64, :] / a[64:65, :]
                og_ref[j * 64:(j + 1) * 64, :] = (o * gz_ref[j * 64:(j + 1) * 64, :].astype(F32)).astype(BF16)

    return init, update, finalize


def _causal_pairs(n_tok, tq, tk):
    qi_l, ki_l = [], []
    for qi in range(n_tok // tq):
        for ki in range(((qi + 1) * tq - 1) // tk + 1):
            qi_l.append(qi)
            ki_l.append(ki)
    return jnp.asarray(qi_l, jnp.int32), jnp.asarray(ki_l, jnp.int32)


def _fused_body(is_diff, tq, tk, dva, pg, ng, n_pairs, n_steps, n_sample_steps, pt_base, n_prompt_in, *refs):
    qi_tab, ki_tab, fixed_tab, pt_ref = refs[:4]
    refs = refs[4:]
    n_sample_in = N_SAMPLE_SEQ_INPUTS + 5
    prompt_in = refs[:n_prompt_in]
    pools = refs[n_prompt_in:n_prompt_in + 5]
    sample_in = refs[n_prompt_in + 5:n_prompt_in + 5 + n_sample_in]
    og_ref, ogs_ref = refs[n_prompt_in + 5 + n_sample_in:n_prompt_in + 5 + n_sample_in + 2]
    scratch = refs[n_prompt_in + 5 + n_sample_in + 2:]
    m_s, acc_s = scratch[:2]
    page_bufs = scratch[2:7]
    page_sem = scratch[7]
    p = pl.program_id(1)
    step = pl.program_id(0) * n_steps + p
    pp = jnp.minimum(p, n_pairs - 1)
    qi = qi_tab[pp]
    ki = ki_tab[pp]
    g = lax.rem(step, ng)
    slot = lax.rem(step, 2)
    a_init, a_update, a_finalize = _attn_parts(is_diff, tq, tk, dva, qi, ki, prompt_in, og_ref, m_s, acc_s)
    s_clear, s_init, s_pages, s_final = _sample_parts(pg, slot, page_bufs, sample_in, ogs_ref, scratch[8:])
    both = jnp.logical_and
    no = jnp.logical_not

    has_prompt = p < n_pairs
    has_sample = step < n_sample_steps
    fixed_shift = fixed_tab[0] == 1
    need_mask = (ki + 1) * tk - 1 > qi * tq

    def start_pages(s, to_slot):
        for c in _page_copies(pg, pools, page_bufs, page_sem, to_slot, lambda i: pt_ref[pt_base + s * pg + i]):
            c.start()

    pl.when(step == 0)(lambda: start_pages(0, 0))
    pl.when(step + 1 < n_sample_steps)(lambda: start_pages(step + 1, 1 - slot))

    @pl.when(has_sample)
    def _():
        for c in _page_copies(pg, pools, page_bufs, page_sem, slot, lambda i: 0):
            c.wait()

    pl.when(step == 0)(s_clear)
    pl.when(both(has_prompt, ki == 0))(a_init)
    pl.when(both(has_sample, g == 0))(s_init)

    usual = both(both(has_prompt, has_sample), both(fixed_shift, no(need_mask)))

    @pl.when(usual)
    def _():
        a_update(False, True)
        s_pages()

    @pl.when(no(usual))
    def _():
        for masked in (True, False):
            for fixed in (True, False):
                c_mask = need_mask if masked else no(need_mask)
                c_fixed = fixed_shift if fixed else no(fixed_shift)
                pl.when(both(has_prompt, both(c_mask, c_fixed)))(functools.partial(a_update, masked, fixed))
        pl.when(has_sample)(s_pages)

    pl.when(both(has_prompt, ki == ((qi + 1) * tq - 1) // tk))(a_finalize)
    pl.when(both(has_sample, g == ng - 1))(s_final)


def _fused_attention(is_diff, qT, k, vT, gzT, g_q, g_k, prompt_extra, n_tok, tq, tk,
                     page_table, seq0, n_seq_call, pools, per_seq, sample_extra, pg):
    dva = DIFF_DVA if is_diff else FOX_DVA
    v_rows = dva if is_diff else 2 * dva
    qi_tab, ki_tab = _causal_pairs(n_tok, tq, tk)
    n_pairs = qi_tab.shape[0]
    n_pages = page_table.shape[1]
    ng = n_pages // pg
    n_sample_steps = n_seq_call * ng
    n_steps = max(n_pairs, -(-n_sample_steps // 4))
    bound = QK_SCALE * HEAD_DIM * jnp.max(jnp.abs(g_q)) * jnp.max(jnp.abs(g_k))
    fixed_tab = (bound <= SHIFT_LIMIT).astype(jnp.int32).reshape(1)
    pt_flat = page_table.reshape(-1)

    def pmap(f):
        def index_map(h, p, qt, kt, fx, pt):
            pp = jnp.minimum(p, n_pairs - 1)
            return f(h, qt[pp], kt[pp])
        return index_map

    def sample_step(h, p):
        return jnp.minimum(h * n_steps + p, n_sample_steps - 1)

    def seq_map(h, p, qt, kt, fx, pt):
        return (seq0 + lax.div(sample_step(h, p), ng), 0, 0)

    def out_seq_map(h, p, qt, kt, fx, pt):
        return (lax.div(sample_step(h, p), ng), 0, 0)

    in_specs = [
        pl.BlockSpec((2 * AUG_DK, tq), pmap(lambda h, q, kk: (h, q))),
        pl.BlockSpec((tk, 2 * AUG_DK), pmap(lambda h, q, kk: (kk, h))),
        pl.BlockSpec((v_rows, tk), pmap(lambda h, q, kk: (h, kk))),
        pl.BlockSpec((128, tq), pmap(lambda h, q, kk: (h, q))),
    ] + [pl.BlockSpec(e.shape, pmap(lambda h, q, kk: (0, 0))) for e in prompt_extra]
    operands = [qT, k, vT, gzT, *prompt_extra]
    n_prompt_in = len(operands)
    for pool in pools:
        in_specs.append(pl.BlockSpec(memory_space=pl.ANY))
        operands.append(pool)
    assert len(per_seq) == N_SAMPLE_SEQ_INPUTS
    for a in per_seq:
        in_specs.append(pl.BlockSpec((1,) + a.shape[1:], seq_map))
        operands.append(a)
    for e in sample_extra:
        in_specs.append(pl.BlockSpec(e.shape, pmap(lambda h, q, kk: (0, 0))))
        operands.append(e)
    scratch = [
        pltpu.VMEM((2, 1, tq), F32), pltpu.VMEM((2, dva, tq), F32),
        pltpu.VMEM((2, pg, 512, 128), F32), pltpu.VMEM((2, pg, 512, 128), F32),
        pltpu.VMEM((2, pg, 8, 128), F32),
        pltpu.VMEM((2, pg, 512, 128), F32), pltpu.VMEM((2, pg, 512, 128), F32),
        pltpu.SemaphoreType.DMA((2,)),
        pltpu.VMEM((64, 512), BF16), pltpu.VMEM((64, 512), BF16),
        pltpu.VMEM((64, 1), F32), pltpu.VMEM((64, 1), F32), pltpu.VMEM((64, 512), F32),
        pltpu.VMEM((64, 1), F32), pltpu.VMEM((64, 1), F32), pltpu.VMEM((64, 128), F32),
        pltpu.VMEM((8, 128), F32),
        pltpu.VMEM((512, 128), F32), pltpu.VMEM((512, 128), F32), pltpu.VMEM((8, 128), F32),
        pltpu.VMEM((512, 128), F32), pltpu.VMEM((512, 128), F32),
    ]
    assert len(scratch) == 8 + N_SAMPLE_SCRATCH
    grid_spec = pltpu.PrefetchScalarGridSpec(
        num_scalar_prefetch=4,
        grid=(4, n_steps),
        in_specs=in_specs,
        out_specs=[pl.BlockSpec((128, tq), pmap(lambda h, q, kk: (h, q))),
                   pl.BlockSpec((1, 8, 1024), out_seq_map)],
        scratch_shapes=scratch,
    )
    return pl.pallas_call(
        functools.partial(_fused_body, is_diff, tq, tk, dva, pg, ng, n_pairs, n_steps,
                          n_sample_steps, seq0 * n_pages, n_prompt_in),
        grid_spec=grid_spec,
        out_shape=[jax.ShapeDtypeStruct((512, n_tok), BF16),
                   jax.ShapeDtypeStruct((n_seq_call, 8, 1024), BF16)],
        compiler_params=pltpu.CompilerParams(dimension_semantics=("arbitrary", "arbitrary"),
                                             vmem_limit_bytes=VMEM_LIMIT),
        name="attn_diff" if is_diff else "attn_fox",
    )(qi_tab, ki_tab, fixed_tab, pt_flat, *operands)


N_SAMPLE_SEQ_INPUTS = 9
N_SAMPLE_SCRATCH = 14


def _page_copies(pg, pools, bufs, sem, slot, page_of):
    return [pltpu.make_async_copy(pool.at[page_of(i)], buf.at[slot, i], sem.at[slot])
            for i in range(pg) for pool, buf in zip(pools, bufs)]


def _sample_parts(pg, slot, page_bufs, in_refs, og_ref, scratch):
    fk_b, fv_b, lf_b, dk_b, dv_b = page_bufs
    (qf_ref, qd_ref, kfn_ref, vfn_ref, lfn_ref, kdn_ref, vdn_ref, gzf_ref, gzd_ref,
     lq1, lk1, lq2, lk2, gs_ref) = in_refs
    (qbf_s, qbd_s, mf_s, lf_s, accf_s, md_s, ld_s, accd_s, off_s,
     nk_s, nv_s, nl_s, ndk_s, ndv_s) = scratch

    def blockdiag():
        row = lax.broadcasted_iota(jnp.int32, (64, 512), 0)
        col = lax.broadcasted_iota(jnp.int32, (64, 512), 1)
        return (row >> 3) == (col >> 6)

    def init():
        qf = jnp.concatenate([qf_ref[0]] * 8, axis=0)
        qd = jnp.concatenate([qd_ref[0]] * 8, axis=0)
        qbf_s[...] = jnp.where(blockdiag(), qf, 0.0).astype(BF16)
        qbd_s[...] = jnp.where(blockdiag(), qd, 0.0).astype(BF16)
        mf_s[...] = jnp.full(mf_s.shape, NEG, F32)
        md_s[...] = jnp.full(md_s.shape, NEG, F32)
        lf_s[...] = jnp.zeros(lf_s.shape, F32)
        ld_s[...] = jnp.zeros(ld_s.shape, F32)
        accf_s[...] = jnp.zeros(accf_s.shape, F32)
        accd_s[...] = jnp.zeros(accd_s.shape, F32)
        off_s[...] = jnp.zeros(off_s.shape, F32)

    def attend(pages, mask):
        n = len(pages)
        cat = lambda xs, axis: xs[0] if n == 1 else jnp.concatenate(xs, axis=axis)
        cw = _lane_cumsum(cat([pg_[2] for pg_ in pages], 0))
        off = off_s[:, 0:1]
        cs = []
        for i in range(n):
            cwi = cw[i * 8:(i + 1) * 8, :]
            c = (off + cwi) * LOG2E
            cs.append(jnp.broadcast_to(c[:, None, :], (8, 8, 128)).reshape(64, 128))
            off = off + cwi[:, 127:128]
        off_s[...] = jnp.broadcast_to(off, off_s.shape)

        kf = cat([pg_[0].astype(BF16) for pg_ in pages], 1)
        s_f = jnp.dot(qbf_s[...], kf, preferred_element_type=F32) - cat(cs, 1)
        kd = cat([pg_[3].astype(BF16) for pg_ in pages], 1)
        s_d = jnp.dot(qbd_s[...], kd, preferred_element_type=F32)
        if mask is not None:
            s_f = jnp.where(mask, s_f, NEG)
            s_d = jnp.where(mask, s_d, NEG)

        m_old = mf_s[...]
        m_new = jnp.maximum(m_old, jnp.max(s_f, axis=1, keepdims=True))
        alpha = jnp.exp2(m_old - m_new)
        pr = jnp.exp2(s_f - m_new)
        lf_s[...] = alpha * lf_s[...] + jnp.sum(pr, axis=1, keepdims=True)
        mf_s[...] = m_new
        vf = cat([pg_[1].astype(BF16) for pg_ in pages], 1)
        pv = lax.dot_general(pr.astype(BF16), vf, (((1,), (1,)), ((), ())), preferred_element_type=F32)
        accf_s[...] = alpha * accf_s[...] + pv

        m_old = md_s[...]
        m_new = jnp.maximum(m_old, jnp.max(s_d, axis=1, keepdims=True))
        alpha = jnp.exp2(m_old - m_new)
        pr = jnp.exp2(s_d - m_new)
        ld_s[...] = alpha * ld_s[...] + jnp.sum(pr, axis=1, keepdims=True)
        md_s[...] = m_new
        pb = pr.astype(BF16)
        for h in range(DIFF_HEADS):
            rows = slice(h * 16, (h + 1) * 16)
            vh = cat([pg_[4][pl.ds(h, 128, stride=4), :].astype(BF16) for pg_ in pages], 0)
            pv = jnp.dot(pb[rows, :], vh, preferred_element_type=F32)
            accd_s[rows, :] = alpha[rows, :] * accd_s[rows, :] + pv

    def pages():
        attend([(fk_b[slot, i], fv_b[slot, i], lf_b[slot, i], dk_b[slot, i], dv_b.at[slot, i])
                for i in range(pg)], None)

    def clear_new_page():
        nk_s[...] = jnp.zeros(nk_s.shape, F32)
        nv_s[...] = jnp.zeros(nv_s.shape, F32)
        nl_s[...] = jnp.zeros(nl_s.shape, F32)
        ndk_s[...] = jnp.zeros(ndk_s.shape, F32)
        ndv_s[...] = jnp.zeros(ndv_s.shape, F32)

    def final():
        nk_s[:, 0:8] = kfn_ref[0]
        nv_s[:, 0:8] = vfn_ref[0]
        nl_s[:, 0:8] = lfn_ref[0]
        ndk_s[:, 0:8] = kdn_ref[0]
        ndv_s[0:32, :] = vdn_ref[0]
        r = lax.broadcasted_iota(jnp.int32, (64, 128), 0)
        c = lax.broadcasted_iota(jnp.int32, (64, 128), 1)
        attend([(nk_s[...], nv_s[...], nl_s[...], ndk_s[...], ndv_s)], c <= (r & 7))

        of = jnp.where(blockdiag(), accf_s[...] / lf_s[...], 0.0)
        of = jnp.sum(of.reshape(8, 8, 512), axis=0)
        ogf = of * gzf_ref[0]
        lam = _diff_lambda(lq1, lk1, lq2, lk2)
        on = accd_s[...] / ld_s[...]
        outs = [ogf]
        for h in range(DIFF_HEADS):
            od = on[h * 16:h * 16 + 8, :] - lam * on[h * 16 + 8:h * 16 + 16, :]
            ms = jnp.mean(od * od, axis=-1, keepdims=True)
            y = (od * lax.rsqrt(ms + NORM_EPS)) * gs_ref[...] * (1.0 - LAMBDA_INIT)
            outs.append(y * gzd_ref[0][:, h * 128:(h + 1) * 128])
        og_ref[0] = jnp.concatenate(outs, axis=1).astype(BF16)

    return clear_new_page, init, pages, final


def _out_t_body(x_ref, of_ref, od_ref, woT_ref, o_ref):
    y_t = jnp.dot(woT_ref[:, 0:512], of_ref[...], preferred_element_type=F32)
    y_t = y_t + jnp.dot(woT_ref[:, 512:1024], od_ref[...], preferred_element_type=F32)
    o_ref[...] = x_ref[...] + y_t.T


def _out_proj_t(x2d, ogf_t, ogd_t, wo_t, tm):
    n_tok = x2d.shape[0]
    return pl.pallas_call(
        _out_t_body,
        grid=(n_tok // tm,),
        in_specs=[pl.BlockSpec((tm, D_MODEL), lambda i: (i, 0)),
                  pl.BlockSpec((512, tm), lambda i: (0, i)),
                  pl.BlockSpec((512, tm), lambda i: (0, i)),
                  pl.BlockSpec((D_MODEL, D_MODEL), lambda i: (0, 0))],
        out_specs=pl.BlockSpec((tm, D_MODEL), lambda i: (i, 0)),
        out_shape=jax.ShapeDtypeStruct((n_tok, D_MODEL), F32),
        compiler_params=pltpu.CompilerParams(dimension_semantics=("arbitrary",),
                                             vmem_limit_bytes=VMEM_LIMIT),
        name="out_proj_prompt",
    )(x2d, ogf_t, ogd_t, wo_t)


def _out_body(x_ref, og_ref, wo_ref, o_ref):
    o_ref[...] = x_ref[...] + jnp.dot(og_ref[...], wo_ref[...], preferred_element_type=F32)


def _out_proj(x2d, og, wo, tm):
    n_tok = x2d.shape[0]
    return pl.pallas_call(
        _out_body,
        grid=(n_tok // tm,),
        in_specs=[pl.BlockSpec((tm, D_MODEL), lambda i: (i, 0)),
                  pl.BlockSpec((tm, D_MODEL), lambda i: (i, 0)),
                  pl.BlockSpec((D_MODEL, D_MODEL), lambda i: (0, 0))],
        out_specs=pl.BlockSpec((tm, D_MODEL), lambda i: (i, 0)),
        out_shape=jax.ShapeDtypeStruct((n_tok, D_MODEL), F32),
        compiler_params=pltpu.CompilerParams(dimension_semantics=("arbitrary",),
                                             vmem_limit_bytes=VMEM_LIMIT),
        name="out_proj_sample",
    )(x2d, og, wo)


def kernel(x_prompt, x_sample, cache_fox_k, cache_fox_v, cache_fox_logf, cache_diff_k, cache_diff_v,
           page_table, g_norm, w_in, b_f, g_fox_q, g_fox_k, g_diff_q, g_diff_k,
           lam_q1, lam_k1, lam_q2, lam_k2, g_subln, w_out):
    seq = x_prompt.shape[1]
    n_seq, dec = x_sample.shape[0], x_sample.shape[1]
    past = page_table.shape[1] * PAGE_SIZE
    n_pool = cache_fox_k.shape[1]

    w = w_in[0]
    sizes = [FOX_WIDTH, FOX_WIDTH, FOX_WIDTH, FOX_HEADS, FOX_WIDTH, DIFF_QK_WIDTH, DIFF_QK_WIDTH,
             DIFF_WIDTH, DIFF_WIDTH]
    offs = [0]
    for s in sizes:
        offs.append(offs[-1] + s)
    col = lambda k: w[:, offs[k]:offs[k + 1]]
    w_t = jnp.concatenate(
        [col(0), col(1), col(2), col(4), col(5), col(6), col(7), col(8), col(3),
         jnp.zeros((D_MODEL, 8), w.dtype)], axis=1).T.astype(BF16)
    w_dv = col(7).astype(BF16)
    half = ROT_DIM // 2
    inv = (ROPE_THETA ** (-jnp.arange(half, dtype=F32) * 2.0 / ROT_DIM)).reshape(half, 1)
    colv = lambda a: a[0].astype(F32).reshape(-1, 1)
    params = (g_norm.astype(F32), w_t, w_dv, colv(b_f), colv(g_fox_q), colv(g_fox_k),
              colv(g_diff_q), colv(g_diff_k), inv)
    lam_params = (lam_q1.astype(F32), lam_k1.astype(F32), lam_q2.astype(F32), lam_k2.astype(F32))
    wo = w_out[0].astype(BF16)

    xp2 = x_prompt[0]
    (fk_t, fv_t, lf_t, dk_t, dv_p, qfa_t, kfa, vfa_t, gzf_t, qd_t, kd, vda_t, gzd_t) = _project(
        xp2, params, True, 256, 0, 256, 1)

    xs_tb = jnp.transpose(x_sample, (1, 0, 2)).reshape(dec * n_seq, D_MODEL)
    (sfk, sfv, slf, sdk, sdv, sfq, sdq, sgzf, sgzd) = _project(
        xs_tb, params, False, n_seq, past, 1, 0)
    bt = lambda a: jnp.transpose(a, (2, 0, 1))
    bf = lambda a: jnp.transpose(a, (2, 1, 0))
    per_seq = (bt(sfq), bt(sdq), bf(sfk), bf(sfv), bf(slf), bf(sdk),
               jnp.transpose(sdv, (1, 0, 2)).reshape(n_seq, dec * DIFF_HEADS, DIFF_V_DIM),
               bt(sgzf), bt(sgzd))
    pools = (
        jnp.transpose(cache_fox_k[0], (0, 2, 3, 1)).reshape(n_pool, 512, PAGE_SIZE),
        jnp.transpose(cache_fox_v[0], (0, 2, 3, 1)).reshape(n_pool, 512, PAGE_SIZE),
        jnp.transpose(cache_fox_logf[0], (0, 2, 1)),
        jnp.transpose(cache_diff_k[0], (0, 2, 3, 4, 1)).reshape(n_pool, 512, PAGE_SIZE),
        cache_diff_v[0].reshape(n_pool, PAGE_SIZE * DIFF_HEADS, DIFF_V_DIM),
    )

    tq = tk = 1024
    n_a = (n_seq + 1) // 2
    sample_extra = lam_params + (g_subln.astype(F32),)
    ogf_t, og_sa = _fused_attention(False, qfa_t, kfa, vfa_t, gzf_t, g_fox_q, g_fox_k, (), seq, tq, tk,
                                    page_table, 0, n_a, pools, per_seq, sample_extra, SAMPLE_PAGES_PER_STEP)
    ogd_t, og_sb = _fused_attention(True, qd_t, kd, vda_t, gzd_t, g_diff_q, g_diff_k,
                                    lam_params + (colv(g_subln),), seq, tq, tk,
                                    page_table, n_a, n_seq - n_a, pools, per_seq, sample_extra,
                                    SAMPLE_PAGES_PER_STEP)
    og_s = jnp.concatenate([og_sa, og_sb], axis=0)
    y_prompt = _out_proj_t(xp2, ogf_t, ogd_t, wo.T, 256)[None]
    y_sample = _out_proj(x_sample.reshape(n_seq * dec, D_MODEL), og_s.reshape(n_seq * dec, D_MODEL),
                         wo, 256).reshape(n_seq, dec, D_MODEL)

    new_fk_p = jnp.transpose(fk_t.reshape(FOX_HEADS, HEAD_DIM, seq), (2, 0, 1))[None, None]
    new_fv_p = jnp.transpose(fv_t.reshape(FOX_HEADS, HEAD_DIM, seq), (2, 0, 1))[None, None]
    new_fl_p = jnp.transpose(lf_t, (1, 0))[None, None]
    new_dk_p = jnp.transpose(dk_t.reshape(DIFF_HEADS, 2, DIFF_QK_DIM, seq), (3, 0, 1, 2))[None, None]
    new_dv_p = dv_p.reshape(seq, DIFF_HEADS, DIFF_V_DIM)[None, None]
    new_fk_s = jnp.transpose(sfk.reshape(dec, FOX_HEADS, HEAD_DIM, n_seq), (3, 0, 1, 2))[None]
    new_fv_s = jnp.transpose(sfv.reshape(dec, FOX_HEADS, HEAD_DIM, n_seq), (3, 0, 1, 2))[None]
    new_fl_s = jnp.transpose(slf, (2, 0, 1))[None]
    new_dk_s = jnp.transpose(sdk.reshape(dec, DIFF_HEADS, 2, DIFF_QK_DIM, n_seq), (4, 0, 1, 2, 3))[None]
    new_dv_s = jnp.transpose(sdv, (1, 0, 2)).reshape(n_seq, dec, DIFF_HEADS, DIFF_V_DIM)[None]
    return (y_prompt, y_sample, new_fk_p, new_fv_p, new_fl_p, new_dk_p, new_dv_p,
            new_fk_s, new_fv_s, new_fl_s, new_dk_s, new_dv_s)
```

```python
import functools

import jax
import jax.numpy as jnp
from jax import lax
from jax.experimental import pallas as pl
from jax.experimental.pallas import tpu as pltpu

D_MODEL = 1024
HEAD_DIM = 64
FOX_HEADS = 8
FOX_WIDTH = FOX_HEADS * HEAD_DIM
DIFF_HEADS = 4
DIFF_QK_DIM = 64
DIFF_V_DIM = 2 * DIFF_QK_DIM
DIFF_QK_WIDTH = DIFF_HEADS * 2 * DIFF_QK_DIM
DIFF_WIDTH = DIFF_HEADS * DIFF_V_DIM
PAGE_SIZE = 128
ROPE_THETA = 500000.0
ROT_DIM = DIFF_QK_DIM // 4
NORM_EPS = 1e-6
LAMBDA_INIT = 0.2

LOG2E = 1.4426950408889634
QK_SCALE = LOG2E * HEAD_DIM ** -0.5
NEG = -1e30

F32 = jnp.float32
BF16 = jnp.bfloat16

AUG_DK = 128
SHIFT_LIMIT = 60.0
FOX_DVA = 80
DIFF_DVA = 144

VMEM_LIMIT = 56 * 1024 * 1024
SAMPLE_PAGES_PER_STEP = 8
PROMPT_COL_TILE = 512
PROMPT_ROW_TILE = 1024

R_FQ, R_FK, R_FV, R_FZ, R_DQ, R_DK, R_DV, R_DZ, R_FF = (i * 512 for i in range(9))
WT_ROWS = R_FF + 16


def _lane_cumsum(x):
    lane = lax.broadcasted_iota(jnp.int32, x.shape, 1)
    sh = 1
    while sh < x.shape[1]:
        x = x + jnp.where(lane >= sh, pltpu.roll(x, sh, 1), 0.0)
        sh *= 2
    return x


def _silu(z):
    return z * (1.0 / (1.0 + jnp.exp(-z)))


def _log_sigmoid(z):
    return jnp.minimum(z, 0.0) - jnp.log1p(jnp.exp(-jnp.abs(z)))


def _diff_lambda(lam_ref):
    a = jnp.sum(lam_ref[0:1, :] * lam_ref[1:2, :], axis=-1, keepdims=True)
    b = jnp.sum(lam_ref[2:3, :] * lam_ref[3:4, :], axis=-1, keepdims=True)
    return jnp.exp(a) - jnp.exp(b) + LAMBDA_INIT


def _proj_body(is_prompt, tm, pos_base, pos_tile, pos_lane, *refs):
    (x_ref, gn_ref, wT_ref, wdv_ref, bf_ref, gfq_ref, gfk_ref, gdq_ref, gdk_ref, inv_ref) = refs[:10]
    rest = refs[10:]
    i = pl.program_id(0)

    x = x_ref[...]
    ms = jnp.mean(x * x, axis=-1, keepdims=True)
    xn = (x * lax.rsqrt(ms + NORM_EPS)) * gn_ref[...]
    xb = xn.astype(BF16)

    def proj_t(r0, n):
        return lax.dot_general(wT_ref[r0:r0 + n, :], xb, (((1,), (1,)), ((), ())),
                               preferred_element_type=F32)

    def head_norm(h_t, g_ref):
        y = h_t.reshape(8, 64, tm)
        m = jnp.mean(y * y, axis=1, keepdims=True)
        return (y * lax.rsqrt(m + NORM_EPS)) * g_ref[...][None]

    pos = pos_base + i * pos_tile + lax.broadcasted_iota(jnp.int32, (1, tm), 1) * pos_lane
    ang = pos.astype(F32) * inv_ref[...]
    cos = jnp.cos(ang)[None]
    sin = jnp.sin(ang)[None]

    def rope(y):
        x1 = y[:, 0:8, :]
        x2 = y[:, 8:16, :]
        return jnp.concatenate([x1 * cos - x2 * sin, x2 * cos + x1 * sin, y[:, 16:, :]], axis=1)

    fq = head_norm(proj_t(R_FQ, 512), gfq_ref)
    fk = head_norm(proj_t(R_FK, 512), gfk_ref)
    fv = proj_t(R_FV, 512)
    lf = _log_sigmoid(proj_t(R_FF, 16)[0:8, :] + bf_ref[...])
    gzf = _silu(proj_t(R_FZ, 512))
    dq = rope(head_norm(proj_t(R_DQ, 512), gdq_ref))
    dk = rope(head_norm(proj_t(R_DK, 512), gdk_ref))
    dv_t = proj_t(R_DV, 512)
    gzd = _silu(proj_t(R_DZ, 512))
    dv_rm = jnp.dot(xb, wdv_ref[...], preferred_element_type=F32)

    fk2 = fk.reshape(512, tm)
    dk2 = dk.reshape(512, tm)
    fq2 = fq.reshape(512, tm) * QK_SCALE
    dq2 = dq.reshape(512, tm) * QK_SCALE

    if not is_prompt:
        (fk_o, fv_o, lf_o, dk_o, dv_o, fq_o, dq_o, gzf_o, gzd_o) = rest
        fk_o[...] = fk2
        fv_o[...] = fv
        lf_o[...] = lf
        dk_o[...] = dk2
        dv_o[...] = dv_rm
        fq_o[...] = fq2
        dq_o[...] = dq2
        gzf_o[...] = gzf
        gzd_o[...] = gzd
        return

    (fk_o, fv_o, lf_o, dk_o, dv_o, qfa_o, kfa_o, vfa_o, gzf_o, qd_o, kd_o, vda_o, gzd_o, carry) = rest
    fk_o[...] = fk2
    fv_o[...] = fv
    lf_o[...] = lf
    dk_o[...] = dk2
    for h in range(DIFF_HEADS):
        dv_o[pl.ds(h, tm, stride=DIFF_HEADS), :] = dv_rm[:, h * DIFF_V_DIM:(h + 1) * DIFF_V_DIM]
    gzf_o[...] = gzf.astype(BF16)
    gzd_o[...] = gzd.astype(BF16)

    @pl.when(i == 0)
    def _():
        carry[...] = jnp.zeros_like(carry)

    run = carry[:, 0:1]
    chunks = []
    for j in range(tm // 128):
        cj = run + _lane_cumsum(lf[:, j * 128:(j + 1) * 128])
        run = cj[:, 127:128]
        chunks.append(cj)
    carry[...] = jnp.broadcast_to(run, carry.shape)
    c2 = jnp.concatenate(chunks, axis=1) * LOG2E

    def split3(v):
        hi = v.astype(BF16).astype(F32)
        r1 = v - hi
        mid = r1.astype(BF16).astype(F32)
        return hi, mid, (r1 - mid).astype(BF16).astype(F32)

    rowid = lax.broadcasted_iota(jnp.int32, (8, tm), 0)

    def slab(pieces, h):
        return jnp.where(rowid == 0, pieces[0][h:h + 1, :],
                         jnp.where(rowid == 1, pieces[1][h:h + 1, :],
                                   jnp.where(rowid == 2, pieces[2][h:h + 1, :], 0.0)))

    def shift_bound(q2, g_ref):
        n = jnp.sqrt(jnp.sum(jnp.square(q2.reshape(8, 64, tm)), axis=1))
        return n * (8.0 * jnp.max(jnp.abs(g_ref[...]), axis=0, keepdims=True))

    c_pieces = split3(c2)
    mf_pieces = split3(shift_bound(fq2, gfk_ref))
    md_pieces = split3(shift_bound(dq2, gdk_ref))
    ones3 = jnp.where(rowid < 3, 1.0, 0.0)
    zeros8 = jnp.zeros((8, tm), F32)
    zeros40 = jnp.zeros((40, tm), F32)
    one_row16 = jnp.where(lax.broadcasted_iota(jnp.int32, (16, tm), 0) == 0, 1.0, 0.0)

    for h in range(FOX_HEADS):
        cs = slab(c_pieces, h)
        q_aug = jnp.concatenate([fq2[h * 64:(h + 1) * 64, :], cs, ones3, -slab(mf_pieces, h), zeros40], axis=0)
        k_aug = jnp.concatenate([fk2[h * 64:(h + 1) * 64, :], ones3, -cs, ones3, zeros40], axis=0)
        qfa_o[h * AUG_DK:(h + 1) * AUG_DK, :] = q_aug.astype(BF16)
        kfa_o[:, h * AUG_DK:(h + 1) * AUG_DK] = k_aug.T.astype(BF16)
        v_aug = jnp.concatenate([fv[h * 64:(h + 1) * 64, :], one_row16], axis=0)
        vfa_o[h * FOX_DVA:(h + 1) * FOX_DVA, :] = v_aug.astype(BF16)

    for j in range(2 * DIFF_HEADS):
        q_aug = jnp.concatenate([dq2[j * 64:(j + 1) * 64, :], zeros8, zeros8, -slab(md_pieces, j), zeros40], axis=0)
        k_aug = jnp.concatenate([dk2[j * 64:(j + 1) * 64, :], zeros8, zeros8, ones3, zeros40], axis=0)
        qd_o[j * AUG_DK:(j + 1) * AUG_DK, :] = q_aug.astype(BF16)
        kd_o[:, j * AUG_DK:(j + 1) * AUG_DK] = k_aug.T.astype(BF16)

    for h in range(DIFF_HEADS):
        v_aug = jnp.concatenate([dv_t[h * 128:(h + 1) * 128, :], one_row16], axis=0)
        vda_o[h * DIFF_DVA:(h + 1) * DIFF_DVA, :] = v_aug.astype(BF16)


def _project(x2d, params, is_prompt, tm, pos_base, pos_tile, pos_lane):
    n_tok = x2d.shape[0]
    n_tiles = n_tok // tm
    full = lambda a: pl.BlockSpec(a.shape, lambda i: (0,) * a.ndim)
    in_specs = [pl.BlockSpec((tm, D_MODEL), lambda i: (i, 0))] + [full(p) for p in params]
    t_spec = lambda rows: pl.BlockSpec((rows, tm), lambda i: (0, i))
    r_spec = lambda cols: pl.BlockSpec((tm, cols), lambda i: (i, 0))
    if is_prompt:
        sds = jax.ShapeDtypeStruct
        out_shape = [
            sds((512, n_tok), F32), sds((512, n_tok), F32), sds((8, n_tok), F32), sds((512, n_tok), F32),
            sds((DIFF_HEADS * n_tok, DIFF_V_DIM), F32),
            sds((8 * AUG_DK, n_tok), BF16), sds((n_tok, 8 * AUG_DK), BF16),
            sds((FOX_HEADS * FOX_DVA, n_tok), BF16), sds((512, n_tok), BF16),
            sds((8 * AUG_DK, n_tok), BF16), sds((n_tok, 8 * AUG_DK), BF16),
            sds((DIFF_HEADS * DIFF_DVA, n_tok), BF16), sds((512, n_tok), BF16),
        ]
        out_specs = [
            t_spec(512), t_spec(512), t_spec(8), t_spec(512),
            pl.BlockSpec((DIFF_HEADS * tm, DIFF_V_DIM), lambda i: (i, 0)),
            t_spec(8 * AUG_DK), r_spec(8 * AUG_DK),
            t_spec(FOX_HEADS * FOX_DVA), t_spec(512),
            t_spec(8 * AUG_DK), r_spec(8 * AUG_DK), t_spec(DIFF_HEADS * DIFF_DVA), t_spec(512),
        ]
        scratch = [pltpu.VMEM((8, 128), F32)]
    else:
        sds = jax.ShapeDtypeStruct
        s3 = lambda rows: pl.BlockSpec((None, rows, tm), lambda i: (i, 0, 0))
        out_shape = [
            sds((n_tiles, 512, tm), F32), sds((n_tiles, 512, tm), F32), sds((n_tiles, 8, tm), F32),
            sds((n_tiles, 512, tm), F32), sds((n_tiles, tm, 512), F32),
            sds((n_tiles, 512, tm), F32), sds((n_tiles, 512, tm), F32),
            sds((n_tiles, 512, tm), F32), sds((n_tiles, 512, tm), F32),
        ]
        out_specs = [s3(512), s3(512), s3(8), s3(512),
                     pl.BlockSpec((None, tm, 512), lambda i: (i, 0, 0)),
                     s3(512), s3(512), s3(512), s3(512)]
        scratch = []
    return pl.pallas_call(
        functools.partial(_proj_body, is_prompt, tm, pos_base, pos_tile, pos_lane),
        grid=(n_tiles,),
        in_specs=in_specs,
        out_specs=out_specs,
        out_shape=out_shape,
        scratch_shapes=scratch,
        compiler_params=pltpu.CompilerParams(dimension_semantics=("arbitrary",),
                                             vmem_limit_bytes=VMEM_LIMIT),
        name="proj_prompt" if is_prompt else "proj_sample",
    )(x2d, *params)


def _attn_parts(is_diff, tq, tk, dva, qi, ki, in_refs, og_ref, m_s, acc_s):
    qT_ref, k_ref, v_ref, gz_ref = in_refs[:4]
    if is_diff:
        lam_ref, gs_ref = in_refs[4:]
    dk = AUG_DK

    def init():
        m_s[...] = jnp.full(m_s.shape, NEG, F32)
        acc_s[...] = jnp.zeros(acc_s.shape, F32)

    def update(masked, fixed):
        if fixed and not masked:
            for j in range(2):
                v0 = 0 if is_diff else j * dva
                for c in range(tq // PROMPT_COL_TILE):
                    cols = slice(c * PROMPT_COL_TILE, (c + 1) * PROMPT_COL_TILE)
                    for r in range(tk // PROMPT_ROW_TILE):
                        rows = slice(r * PROMPT_ROW_TILE, (r + 1) * PROMPT_ROW_TILE)
                        s = jnp.dot(k_ref[rows, j * dk:(j + 1) * dk], qT_ref[j * dk:(j + 1) * dk, cols],
                                    preferred_element_type=F32)
                        acc_s[j, :, cols] += jnp.dot(v_ref[v0:v0 + dva, rows], jnp.exp2(s).astype(BF16),
                                                     preferred_element_type=F32)
            return
        for j in range(2):
            s = jnp.dot(k_ref[:, j * dk:(j + 1) * dk], qT_ref[j * dk:(j + 1) * dk, :],
                        preferred_element_type=F32)
            if masked:
                kpos = ki * tk + lax.broadcasted_iota(jnp.int32, (tk, tq), 0)
                qpos = qi * tq + lax.broadcasted_iota(jnp.int32, (tk, tq), 1)
                s = jnp.where(kpos <= qpos, s, NEG)
            vj = v_ref[...] if is_diff else v_ref[j * dva:(j + 1) * dva, :]
            if fixed:
                acc_s[j] += jnp.dot(vj, jnp.exp2(s).astype(BF16), preferred_element_type=F32)
            else:
                m_old = m_s[j]
                m_new = jnp.maximum(m_old, jnp.max(s, axis=0, keepdims=True))
                alpha = jnp.exp2(m_old - m_new)
                pb = jnp.exp2(s - m_new).astype(BF16)
                acc_s[j] = alpha * acc_s[j] + jnp.dot(vj, pb, preferred_element_type=F32)
                m_s[j] = m_new

    def finalize():
        if is_diff:
            lam = _diff_lambda(lam_ref)
            a1 = acc_s[0]
            a2 = acc_s[1]
            od = a1[0:128, :] / a1[128:129, :] - lam * (a2[0:128, :] / a2[128:129, :])
            ms = jnp.mean(od * od, axis=0, keepdims=True)
            y = (od * lax.rsqrt(ms + NORM_EPS)) * gs_ref[...] * (1.0 - LAMBDA_INIT)
            og_ref[...] = (y * gz_ref[...].astype(F32)).astype(BF16)
        else:
            for j in range(2):
                a = acc_s[j]
                o = a[0:64, :] / a[64:65, :]
                og_ref[j * 64:(j + 1) * 64, :] = (o * gz_ref[j * 64:(j + 1) * 64, :].astype(F32)).astype(BF16)

    return init, update, finalize


def _causal_pairs(n_tok, tq, tk):
    qi_l, ki_l = [], []
    for qi in range(n_tok // tq):
        for ki in range(((qi + 1) * tq - 1) // tk + 1):
            qi_l.append(qi)
            ki_l.append(ki)
    return jnp.asarray(qi_l, jnp.int32), jnp.asarray(ki_l, jnp.int32)


def _fused_body(is_diff, tq, tk, dva, pg, ng, n_pairs, n_steps, n_sample_steps, pt_base, n_prompt_in, *refs):
    qi_tab, ki_tab, fixed_tab, pt_ref = refs[:4]
    refs = refs[4:]
    n_sample_in = N_SAMPLE_SEQ_INPUTS + 2
    prompt_in = refs[:n_prompt_in]
    pools = refs[n_prompt_in:n_prompt_in + 5]
    sample_in = refs[n_prompt_in + 5:n_prompt_in + 5 + n_sample_in]
    og_ref, ogs_ref = refs[n_prompt_in + 5 + n_sample_in:n_prompt_in + 5 + n_sample_in + 2]
    scratch = refs[n_prompt_in + 5 + n_sample_in + 2:]
    m_s, acc_s = scratch[:2]
    page_bufs = scratch[2:7]
    page_sem = scratch[7]
    p = pl.program_id(1)
    step = pl.program_id(0) * n_steps + p
    pp = jnp.minimum(p, n_pairs - 1)
    qi = qi_tab[pp]
    ki = ki_tab[pp]
    g = lax.rem(step, ng)
    slot = lax.rem(step, 2)
    a_init, a_update, a_finalize = _attn_parts(is_diff, tq, tk, dva, qi, ki, prompt_in, og_ref, m_s, acc_s)
    s_clear, s_init, s_pages, s_final = _sample_parts(pg, slot, page_bufs, sample_in, ogs_ref, scratch[8:])
    both = jnp.logical_and
    no = jnp.logical_not

    has_prompt = p < n_pairs
    has_sample = step < n_sample_steps
    fixed_shift = fixed_tab[0] == 1
    need_mask = (ki + 1) * tk - 1 > qi * tq

    def start_pages(s, to_slot):
        for c in _page_copies(pg, pools, page_bufs, page_sem, to_slot, lambda i: pt_ref[pt_base + s * pg + i]):
            c.start()

    pl.when(step == 0)(lambda: start_pages(0, 0))
    pl.when(step + 1 < n_sample_steps)(lambda: start_pages(step + 1, 1 - slot))

    @pl.when(has_sample)
    def _():
        for c in _page_copies(pg, pools, page_bufs, page_sem, slot, lambda i: 0):
            c.wait()

    pl.when(step == 0)(s_clear)
    pl.when(both(has_prompt, ki == 0))(a_init)
    pl.when(both(has_sample, g == 0))(s_init)

    paired = both(both(has_prompt, has_sample), fixed_shift)

    @pl.when(both(paired, no(need_mask)))
    def _():
        a_update(False, True)
        s_pages()

    @pl.when(both(paired, need_mask))
    def _():
        a_update(True, True)
        s_pages()

    @pl.when(no(paired))
    def _():
        for masked in (True, False):
            for fixed in (True, False):
                c_mask = need_mask if masked else no(need_mask)
                c_fixed = fixed_shift if fixed else no(fixed_shift)
                pl.when(both(has_prompt, both(c_mask, c_fixed)))(functools.partial(a_update, masked, fixed))
        pl.when(has_sample)(s_pages)

    pl.when(both(has_prompt, ki == ((qi + 1) * tq - 1) // tk))(a_finalize)
    pl.when(both(has_sample, g == ng - 1))(s_final)


def _fused_attention(is_diff, qT, k, vT, gzT, g_q, g_k, prompt_extra, n_tok, tq, tk,
                     page_table, seq0, n_seq_call, pools, per_seq, sample_extra, pg):
    dva = DIFF_DVA if is_diff else FOX_DVA
    v_rows = dva if is_diff else 2 * dva
    qi_tab, ki_tab = _causal_pairs(n_tok, tq, tk)
    n_pairs = qi_tab.shape[0]
    n_pages = page_table.shape[1]
    ng = n_pages // pg
    n_sample_steps = n_seq_call * ng
    n_steps = max(n_pairs, -(-n_sample_steps // 4))
    bound = QK_SCALE * HEAD_DIM * jnp.max(jnp.abs(g_q)) * jnp.max(jnp.abs(g_k))
    fixed_tab = (bound <= SHIFT_LIMIT).astype(jnp.int32).reshape(1)
    pt_flat = page_table.reshape(-1)

    def pmap(f):
        def index_map(h, p, qt, kt, fx, pt):
            pp = jnp.minimum(p, n_pairs - 1)
            return f(h, qt[pp], kt[pp])
        return index_map

    def sample_step(h, p):
        return jnp.minimum(h * n_steps + p, n_sample_steps - 1)

    def seq_map(h, p, qt, kt, fx, pt):
        return (seq0 + lax.div(sample_step(h, p), ng), 0, 0)

    def out_seq_map(h, p, qt, kt, fx, pt):
        return (lax.div(sample_step(h, p), ng), 0, 0)

    in_specs = [
        pl.BlockSpec((2 * AUG_DK, tq), pmap(lambda h, q, kk: (h, q))),
        pl.BlockSpec((tk, 2 * AUG_DK), pmap(lambda h, q, kk: (kk, h))),
        pl.BlockSpec((v_rows, tk), pmap(lambda h, q, kk: (h, kk))),
        pl.BlockSpec((128, tq), pmap(lambda h, q, kk: (h, q))),
    ] + [pl.BlockSpec(e.shape, pmap(lambda h, q, kk: (0, 0))) for e in prompt_extra]
    operands = [qT, k, vT, gzT, *prompt_extra]
    n_prompt_in = len(operands)
    for pool in pools:
        in_specs.append(pl.BlockSpec(memory_space=pl.ANY))
        operands.append(pool)
    assert len(per_seq) == N_SAMPLE_SEQ_INPUTS
    for a in per_seq:
        in_specs.append(pl.BlockSpec((1,) + a.shape[1:], seq_map))
        operands.append(a)
    for e in sample_extra:
        in_specs.append(pl.BlockSpec(e.shape, pmap(lambda h, q, kk: (0, 0))))
        operands.append(e)
    scratch = [
        pltpu.VMEM((2, 1, tq), F32), pltpu.VMEM((2, dva, tq), F32),
        pltpu.VMEM((2, pg, 512, 128), F32), pltpu.VMEM((2, pg, 512, 128), F32),
        pltpu.VMEM((2, pg, 8, 128), F32),
        pltpu.VMEM((2, pg, 512, 128), F32), pltpu.VMEM((2, pg, 512, 128), F32),
        pltpu.SemaphoreType.DMA((2,)),
        pltpu.VMEM((64, 512), BF16), pltpu.VMEM((64, 512), BF16),
        pltpu.VMEM((64, 1), F32), pltpu.VMEM((64, 1), F32), pltpu.VMEM((64, 512), F32),
        pltpu.VMEM((64, 1), F32), pltpu.VMEM((64, 1), F32), pltpu.VMEM((64, 128), F32),
        pltpu.VMEM((8, 128), F32),
        pltpu.VMEM((512, 128), F32), pltpu.VMEM((512, 128), F32), pltpu.VMEM((8, 128), F32),
        pltpu.VMEM((512, 128), F32), pltpu.VMEM((512, 128), F32),
    ]
    assert len(scratch) == 8 + N_SAMPLE_SCRATCH
    grid_spec = pltpu.PrefetchScalarGridSpec(
        num_scalar_prefetch=4,
        grid=(4, n_steps),
        in_specs=in_specs,
        out_specs=[pl.BlockSpec((128, tq), pmap(lambda h, q, kk: (h, q))),
                   pl.BlockSpec((1, 8, 1024), out_seq_map)],
        scratch_shapes=scratch,
    )
    return pl.pallas_call(
        functools.partial(_fused_body, is_diff, tq, tk, dva, pg, ng, n_pairs, n_steps,
                          n_sample_steps, seq0 * n_pages, n_prompt_in),
        grid_spec=grid_spec,
        out_shape=[jax.ShapeDtypeStruct((512, n_tok), BF16),
                   jax.ShapeDtypeStruct((n_seq_call, 8, 1024), BF16)],
        compiler_params=pltpu.CompilerParams(dimension_semantics=("arbitrary", "arbitrary"),
                                             vmem_limit_bytes=VMEM_LIMIT),
        name="attn_diff" if is_diff else "attn_fox",
    )(qi_tab, ki_tab, fixed_tab, pt_flat, *operands)


N_SAMPLE_SEQ_INPUTS = 4
N_SAMPLE_SCRATCH = 14


def _page_copies(pg, pools, bufs, sem, slot, page_of):
    return [pltpu.make_async_copy(pool.at[page_of(i)], buf.at[slot, i], sem.at[slot])
            for i in range(pg) for pool, buf in zip(pools, bufs)]


def _sample_parts(pg, slot, page_bufs, in_refs, og_ref, scratch):
    fk_b, fv_b, lf_b, dk_b, dv_b = page_bufs
    tok_ref, newk_ref, lfn_ref, vdn_ref, lam_ref, gs_ref = in_refs
    (qbf_s, qbd_s, mf_s, lf_s, accf_s, md_s, ld_s, accd_s, off_s,
     nk_s, nv_s, nl_s, ndk_s, ndv_s) = scratch

    def blockdiag():
        row = lax.broadcasted_iota(jnp.int32, (64, 512), 0)
        col = lax.broadcasted_iota(jnp.int32, (64, 512), 1)
        return (row >> 3) == (col >> 6)

    def init():
        qf = jnp.concatenate([tok_ref[0, :, 0:512]] * 8, axis=0)
        qd = jnp.concatenate([tok_ref[0, :, 512:1024]] * 8, axis=0)
        qbf_s[...] = jnp.where(blockdiag(), qf, 0.0).astype(BF16)
        qbd_s[...] = jnp.where(blockdiag(), qd, 0.0).astype(BF16)
        mf_s[...] = jnp.full(mf_s.shape, NEG, F32)
        md_s[...] = jnp.full(md_s.shape, NEG, F32)
        lf_s[...] = jnp.zeros(lf_s.shape, F32)
        ld_s[...] = jnp.zeros(ld_s.shape, F32)
        accf_s[...] = jnp.zeros(accf_s.shape, F32)
        accd_s[...] = jnp.zeros(accd_s.shape, F32)
        off_s[...] = jnp.zeros(off_s.shape, F32)

    def attend(pages, mask):
        n = len(pages)
        cat = lambda xs, axis: xs[0] if n == 1 else jnp.concatenate(xs, axis=axis)
        cw = _lane_cumsum(cat([pg_[2] for pg_ in pages], 0))
        off = off_s[:, 0:1]
        cs = []
        for i in range(n):
            cwi = cw[i * 8:(i + 1) * 8, :]
            c = (off + cwi) * LOG2E
            cs.append(jnp.broadcast_to(c[:, None, :], (8, 8, 128)).reshape(64, 128))
            off = off + cwi[:, 127:128]
        off_s[...] = jnp.broadcast_to(off, off_s.shape)

        kf = cat([pg_[0].astype(BF16) for pg_ in pages], 1)
        s_f = jnp.dot(qbf_s[...], kf, preferred_element_type=F32) - cat(cs, 1)
        kd = cat([pg_[3].astype(BF16) for pg_ in pages], 1)
        s_d = jnp.dot(qbd_s[...], kd, preferred_element_type=F32)
        if mask is not None:
            s_f = jnp.where(mask, s_f, NEG)
            s_d = jnp.where(mask, s_d, NEG)

        m_old = mf_s[...]
        m_new = jnp.maximum(m_old, jnp.max(s_f, axis=1, keepdims=True))
        alpha = jnp.exp2(m_old - m_new)
        pr = jnp.exp2(s_f - m_new)
        lf_s[...] = alpha * lf_s[...] + jnp.sum(pr, axis=1, keepdims=True)
        mf_s[...] = m_new
        vf = cat([pg_[1].astype(BF16) for pg_ in pages], 1)
        pv = lax.dot_general(pr.astype(BF16), vf, (((1,), (1,)), ((), ())), preferred_element_type=F32)
        accf_s[...] = alpha * accf_s[...] + pv

        m_old = md_s[...]
        m_new = jnp.maximum(m_old, jnp.max(s_d, axis=1, keepdims=True))
        alpha = jnp.exp2(m_old - m_new)
        pr = jnp.exp2(s_d - m_new)
        ld_s[...] = alpha * ld_s[...] + jnp.sum(pr, axis=1, keepdims=True)
        md_s[...] = m_new
        pb = pr.astype(BF16)
        for h in range(DIFF_HEADS):
            rows = slice(h * 16, (h + 1) * 16)
            vh = cat([pg_[4][pl.ds(h, 128, stride=4), :].astype(BF16) for pg_ in pages], 0)
            pv = jnp.dot(pb[rows, :], vh, preferred_element_type=F32)
            accd_s[rows, :] = alpha[rows, :] * accd_s[rows, :] + pv

    def pages():
        attend([(fk_b[slot, i], fv_b[slot, i], lf_b[slot, i], dk_b[slot, i], dv_b.at[slot, i])
                for i in range(pg)], None)

    def clear_new_page():
        nk_s[...] = jnp.zeros(nk_s.shape, F32)
        nv_s[...] = jnp.zeros(nv_s.shape, F32)
        nl_s[...] = jnp.zeros(nl_s.shape, F32)
        ndk_s[...] = jnp.zeros(ndk_s.shape, F32)
        ndv_s[...] = jnp.zeros(ndv_s.shape, F32)

    def final():
        nk_s[:, 0:8] = newk_ref[0, 0:512, :]
        nv_s[:, 0:8] = newk_ref[0, 512:1024, :]
        nl_s[:, 0:8] = lfn_ref[0]
        ndk_s[:, 0:8] = newk_ref[0, 1024:1536, :]
        ndv_s[0:32, :] = vdn_ref[0]
        r = lax.broadcasted_iota(jnp.int32, (64, 128), 0)
        c = lax.broadcasted_iota(jnp.int32, (64, 128), 1)
        attend([(nk_s[...], nv_s[...], nl_s[...], ndk_s[...], ndv_s)], c <= (r & 7))

        of = jnp.where(blockdiag(), accf_s[...] / lf_s[...], 0.0)
        of = jnp.sum(of.reshape(8, 8, 512), axis=0)
        ogf = of * tok_ref[0, :, 1024:1536]
        lam = _diff_lambda(lam_ref)
        on = accd_s[...] / ld_s[...]
        outs = [ogf]
        for h in range(DIFF_HEADS):
            od = on[h * 16:h * 16 + 8, :] - lam * on[h * 16 + 8:h * 16 + 16, :]
            ms = jnp.mean(od * od, axis=-1, keepdims=True)
            y = (od * lax.rsqrt(ms + NORM_EPS)) * gs_ref[...] * (1.0 - LAMBDA_INIT)
            outs.append(y * tok_ref[0, :, 1536 + h * 128:1536 + (h + 1) * 128])
        og_ref[0] = jnp.concatenate(outs, axis=1).astype(BF16)

    return clear_new_page, init, pages, final


def _out_t_body(x_ref, of_ref, od_ref, woT_ref, o_ref):
    y_t = jnp.dot(woT_ref[:, 0:512], of_ref[...], preferred_element_type=F32)
    y_t = y_t + jnp.dot(woT_ref[:, 512:1024], od_ref[...], preferred_element_type=F32)
    o_ref[...] = x_ref[...] + y_t.T


def _out_proj_t(x2d, ogf_t, ogd_t, wo_t, tm):
    n_tok = x2d.shape[0]
    return pl.pallas_call(
        _out_t_body,
        grid=(n_tok // tm,),
        in_specs=[pl.BlockSpec((tm, D_MODEL), lambda i: (i, 0)),
                  pl.BlockSpec((512, tm), lambda i: (0, i)),
                  pl.BlockSpec((512, tm), lambda i: (0, i)),
                  pl.BlockSpec((D_MODEL, D_MODEL), lambda i: (0, 0))],
        out_specs=pl.BlockSpec((tm, D_MODEL), lambda i: (i, 0)),
        out_shape=jax.ShapeDtypeStruct((n_tok, D_MODEL), F32),
        compiler_params=pltpu.CompilerParams(dimension_semantics=("arbitrary",),
                                             vmem_limit_bytes=VMEM_LIMIT),
        name="out_proj_prompt",
    )(x2d, ogf_t, ogd_t, wo_t)


def _out_body(x_ref, og_ref, wo_ref, o_ref):
    o_ref[...] = x_ref[...] + jnp.dot(og_ref[...], wo_ref[...], preferred_element_type=F32)


def _out_proj(x2d, og, wo, tm):
    n_tok = x2d.shape[0]
    return pl.pallas_call(
        _out_body,
        grid=(n_tok // tm,),
        in_specs=[pl.BlockSpec((tm, D_MODEL), lambda i: (i, 0)),
                  pl.BlockSpec((tm, D_MODEL), lambda i: (i, 0)),
                  pl.BlockSpec((D_MODEL, D_MODEL), lambda i: (0, 0))],
        out_specs=pl.BlockSpec((tm, D_MODEL), lambda i: (i, 0)),
        out_shape=jax.ShapeDtypeStruct((n_tok, D_MODEL), F32),
        compiler_params=pltpu.CompilerParams(dimension_semantics=("arbitrary",),
                                             vmem_limit_bytes=VMEM_LIMIT),
        name="out_proj_sample",
    )(x2d, og, wo)


def kernel(x_prompt, x_sample, cache_fox_k, cache_fox_v, cache_fox_logf, cache_diff_k, cache_diff_v,
           page_table, g_norm, w_in, b_f, g_fox_q, g_fox_k, g_diff_q, g_diff_k,
           lam_q1, lam_k1, lam_q2, lam_k2, g_subln, w_out):
    seq = x_prompt.shape[1]
    n_seq, dec = x_sample.shape[0], x_sample.shape[1]
    past = page_table.shape[1] * PAGE_SIZE
    n_pool = cache_fox_k.shape[1]

    w = w_in[0]
    sizes = [FOX_WIDTH, FOX_WIDTH, FOX_WIDTH, FOX_HEADS, FOX_WIDTH, DIFF_QK_WIDTH, DIFF_QK_WIDTH,
             DIFF_WIDTH, DIFF_WIDTH]
    offs = [0]
    for s in sizes:
        offs.append(offs[-1] + s)
    col = lambda k: w[:, offs[k]:offs[k + 1]]
    w_t = jnp.concatenate(
        [col(0), col(1), col(2), col(4), col(5), col(6), col(7), col(8), col(3),
         jnp.zeros((D_MODEL, 8), w.dtype)], axis=1).T.astype(BF16)
    w_dv = col(7).astype(BF16)
    half = ROT_DIM // 2
    inv = (ROPE_THETA ** (-jnp.arange(half, dtype=F32) * 2.0 / ROT_DIM)).reshape(half, 1)
    colv = lambda a: a[0].astype(F32).reshape(-1, 1)
    params = (g_norm.astype(F32), w_t, w_dv, colv(b_f), colv(g_fox_q), colv(g_fox_k),
              colv(g_diff_q), colv(g_diff_k), inv)
    lam_params = (jnp.concatenate([lam_q1, lam_k1, lam_q2, lam_k2], axis=0).astype(F32),)
    wo = w_out[0].astype(BF16)

    xp2 = x_prompt[0]
    (fk_t, fv_t, lf_t, dk_t, dv_p, qfa_t, kfa, vfa_t, gzf_t, qd_t, kd, vda_t, gzd_t) = _project(
        xp2, params, True, 256, 0, 256, 1)

    xs_tb = jnp.transpose(x_sample, (1, 0, 2)).reshape(dec * n_seq, D_MODEL)
    (sfk, sfv, slf, sdk, sdv, sfq, sdq, sgzf, sgzd) = _project(
        xs_tb, params, False, n_seq, past, 1, 0)
    bt = lambda a: jnp.transpose(a, (2, 0, 1))
    bf = lambda a: jnp.transpose(a, (2, 1, 0))
    per_seq = (bt(jnp.concatenate([sfq, sdq, sgzf, sgzd], axis=1)),
               bf(jnp.concatenate([sfk, sfv, sdk], axis=1)),
               bf(slf),
               jnp.transpose(sdv, (1, 0, 2)).reshape(n_seq, dec * DIFF_HEADS, DIFF_V_DIM))
    pools = (
        jnp.transpose(cache_fox_k[0], (0, 2, 3, 1)).reshape(n_pool, 512, PAGE_SIZE),
        jnp.transpose(cache_fox_v[0], (0, 2, 3, 1)).reshape(n_pool, 512, PAGE_SIZE),
        jnp.transpose(cache_fox_logf[0], (0, 2, 1)),
        jnp.transpose(cache_diff_k[0], (0, 2, 3, 4, 1)).reshape(n_pool, 512, PAGE_SIZE),
        cache_diff_v[0].reshape(n_pool, PAGE_SIZE * DIFF_HEADS, DIFF_V_DIM),
    )

    tq = tk = 1024
    n_a = (n_seq + 1) // 2
    sample_extra = lam_params + (g_subln.astype(F32),)
    ogf_t, og_sa = _fused_attention(False, qfa_t, kfa, vfa_t, gzf_t, g_fox_q, g_fox_k, (), seq, tq, tk,
                                    page_table, 0, n_a, pools, per_seq, sample_extra, SAMPLE_PAGES_PER_STEP)
    ogd_t, og_sb = _fused_attention(True, qd_t, kd, vda_t, gzd_t, g_diff_q, g_diff_k,
                                    lam_params + (colv(g_subln),), seq, tq, tk,
                                    page_table, n_a, n_seq - n_a, pools, per_seq, sample_extra,
                                    SAMPLE_PAGES_PER_STEP)
    og_s = jnp.concatenate([og_sa, og_sb], axis=0)
    y_prompt = _out_proj_t(xp2, ogf_t, ogd_t, wo.T, 512)[None]
    y_sample = _out_proj(x_sample.reshape(n_seq * dec, D_MODEL), og_s.reshape(n_seq * dec, D_MODEL),
                         wo, 256).reshape(n_seq, dec, D_MODEL)

    new_fk_p = jnp.transpose(fk_t.reshape(FOX_HEADS, HEAD_DIM, seq), (2, 0, 1))[None, None]
    new_fv_p = jnp.transpose(fv_t.reshape(FOX_HEADS, HEAD_DIM, seq), (2, 0, 1))[None, None]
    new_fl_p = jnp.transpose(lf_t, (1, 0))[None, None]
    new_dk_p = jnp.transpose(dk_t.reshape(DIFF_HEADS, 2, DIFF_QK_DIM, seq), (3, 0, 1, 2))[None, None]
    new_dv_p = dv_p.reshape(seq, DIFF_HEADS, DIFF_V_DIM)[None, None]
    new_fk_s = jnp.transpose(sfk.reshape(dec, FOX_HEADS, HEAD_DIM, n_seq), (3, 0, 1, 2))[None]
    new_fv_s = jnp.transpose(sfv.reshape(dec, FOX_HEADS, HEAD_DIM, n_seq), (3, 0, 1, 2))[None]
    new_fl_s = jnp.transpose(slf, (2, 0, 1))[None]
    new_dk_s = jnp.transpose(sdk.reshape(dec, DIFF_HEADS, 2, DIFF_QK_DIM, n_seq), (4, 0, 1, 2, 3))[None]
    new_dv_s = jnp.transpose(sdv, (1, 0, 2)).reshape(n_seq, dec, DIFF_HEADS, DIFF_V_DIM)[None]
    return (y_prompt, y_sample, new_fk_p, new_fv_p, new_fl_p, new_dk_p, new_dv_p,
            new_fk_s, new_fv_s, new_fl_s, new_dk_s, new_dv_s)
```

```python
import functools

import jax
import jax.numpy as jnp
from jax import lax
from jax.experimental import pallas as pl
from jax.experimental.pallas import tpu as pltpu

D_MODEL = 1024
HEAD_DIM = 64
FOX_HEADS = 8
FOX_WIDTH = FOX_HEADS * HEAD_DIM
DIFF_HEADS = 4
DIFF_QK_DIM = 64
DIFF_V_DIM = 2 * DIFF_QK_DIM
DIFF_QK_WIDTH = DIFF_HEADS * 2 * DIFF_QK_DIM
DIFF_WIDTH = DIFF_HEADS * DIFF_V_DIM
PAGE_SIZE = 128
ROPE_THETA = 500000.0
ROT_DIM = DIFF_QK_DIM // 4
NORM_EPS = 1e-6
LAMBDA_INIT = 0.2

LOG2E = 1.4426950408889634
QK_SCALE = LOG2E * HEAD_DIM ** -0.5
NEG = -1e30

F32 = jnp.float32
BF16 = jnp.bfloat16

AUG_DK = 128
SHIFT_LIMIT = 60.0
FOX_DVA = 80
DIFF_DVA = 144

VMEM_LIMIT = 56 * 1024 * 1024
SAMPLE_PAGES_PER_STEP = 8
PROMPT_COL_TILE = 512
PROMPT_ROW_TILE = 1024

R_FQ, R_FK, R_FV, R_FZ, R_DQ, R_DK, R_DV, R_DZ, R_FF = (i * 512 for i in range(9))
WT_ROWS = R_FF + 16


def _lane_cumsum(x):
    lane = lax.broadcasted_iota(jnp.int32, x.shape, 1)
    sh = 1
    while sh < x.shape[1]:
        x = x + jnp.where(lane >= sh, pltpu.roll(x, sh, 1), 0.0)
        sh *= 2
    return x


def _silu(z):
    return z * (1.0 / (1.0 + jnp.exp(-z)))


def _log_sigmoid(z):
    return jnp.minimum(z, 0.0) - jnp.log1p(jnp.exp(-jnp.abs(z)))


def _diff_lambda(lam_ref):
    a = jnp.sum(lam_ref[0:1, :] * lam_ref[1:2, :], axis=-1, keepdims=True)
    b = jnp.sum(lam_ref[2:3, :] * lam_ref[3:4, :], axis=-1, keepdims=True)
    return jnp.exp(a) - jnp.exp(b) + LAMBDA_INIT


def _proj_body(is_prompt, tm, pos_base, pos_tile, pos_lane, *refs):
    (x_ref, gn_ref, wT_ref, wdv_ref, bf_ref, gfq_ref, gfk_ref, gdq_ref, gdk_ref, inv_ref) = refs[:10]
    rest = refs[10:]
    i = pl.program_id(0)

    x = x_ref[...]
    ms = jnp.mean(x * x, axis=-1, keepdims=True)
    xn = (x * lax.rsqrt(ms + NORM_EPS)) * gn_ref[...]
    xb = xn.astype(BF16)

    def proj_t(r0, n):
        return lax.dot_general(wT_ref[r0:r0 + n, :], xb, (((1,), (1,)), ((), ())),
                               preferred_element_type=F32)

    def head_norm(h_t, g_ref):
        y = h_t.reshape(8, 64, tm)
        m = jnp.mean(y * y, axis=1, keepdims=True)
        return (y * lax.rsqrt(m + NORM_EPS)) * g_ref[...][None]

    pos = pos_base + i * pos_tile + lax.broadcasted_iota(jnp.int32, (1, tm), 1) * pos_lane
    ang = pos.astype(F32) * inv_ref[...]
    cos = jnp.cos(ang)[None]
    sin = jnp.sin(ang)[None]

    def rope(y):
        x1 = y[:, 0:8, :]
        x2 = y[:, 8:16, :]
        return jnp.concatenate([x1 * cos - x2 * sin, x2 * cos + x1 * sin, y[:, 16:, :]], axis=1)

    fq = head_norm(proj_t(R_FQ, 512), gfq_ref)
    fk = head_norm(proj_t(R_FK, 512), gfk_ref)
    fv = proj_t(R_FV, 512)
    lf = _log_sigmoid(proj_t(R_FF, 16)[0:8, :] + bf_ref[...])
    gzf = _silu(proj_t(R_FZ, 512))
    dq = rope(head_norm(proj_t(R_DQ, 512), gdq_ref))
    dk = rope(head_norm(proj_t(R_DK, 512), gdk_ref))
    dv_t = proj_t(R_DV, 512)
    gzd = _silu(proj_t(R_DZ, 512))
    dv_rm = jnp.dot(xb, wdv_ref[...], preferred_element_type=F32)

    fk2 = fk.reshape(512, tm)
    dk2 = dk.reshape(512, tm)
    fq2 = fq.reshape(512, tm) * QK_SCALE
    dq2 = dq.reshape(512, tm) * QK_SCALE

    if not is_prompt:
        (fk_o, fv_o, lf_o, dk_o, dv_o, fq_o, dq_o, gzf_o, gzd_o) = rest
        fk_o[...] = fk2
        fv_o[...] = fv
        lf_o[...] = lf
        dk_o[...] = dk2
        dv_o[...] = dv_rm
        fq_o[...] = fq2
        dq_o[...] = dq2
        gzf_o[...] = gzf
        gzd_o[...] = gzd
        return

    (fk_o, fv_o, lf_o, dk_o, dv_o, qfa_o, kfa_o, vfa_o, gzf_o, qd_o, kd_o, vda_o, gzd_o, carry) = rest
    fk_o[...] = fk2
    fv_o[...] = fv
    lf_o[...] = lf
    dk_o[...] = dk2
    for h in range(DIFF_HEADS):
        dv_o[pl.ds(h, tm, stride=DIFF_HEADS), :] = dv_rm[:, h * DIFF_V_DIM:(h + 1) * DIFF_V_DIM]
    gzf_o[...] = gzf.astype(BF16)
    gzd_o[...] = gzd.astype(BF16)

    @pl.when(i == 0)
    def _():
        carry[...] = jnp.zeros_like(carry)

    run = carry[:, 0:1]
    chunks = []
    for j in range(tm // 128):
        cj = run + _lane_cumsum(lf[:, j * 128:(j + 1) * 128])
        run = cj[:, 127:128]
        chunks.append(cj)
    carry[...] = jnp.broadcast_to(run, carry.shape)
    c2 = jnp.concatenate(chunks, axis=1) * LOG2E

    def split3(v):
        hi = v.astype(BF16).astype(F32)
        r1 = v - hi
        mid = r1.astype(BF16).astype(F32)
        return hi, mid, (r1 - mid).astype(BF16).astype(F32)

    rowid = lax.broadcasted_iota(jnp.int32, (8, tm), 0)

    def slab(pieces, h):
        return jnp.where(rowid == 0, pieces[0][h:h + 1, :],
                         jnp.where(rowid == 1, pieces[1][h:h + 1, :],
                                   jnp.where(rowid == 2, pieces[2][h:h + 1, :], 0.0)))

    def shift_bound(q2, g_ref):
        n = jnp.sqrt(jnp.sum(jnp.square(q2.reshape(8, 64, tm)), axis=1))
        return n * (8.0 * jnp.max(jnp.abs(g_ref[...]), axis=0, keepdims=True))

    c_pieces = split3(c2)
    mf_pieces = split3(shift_bound(fq2, gfk_ref))
    md_pieces = split3(shift_bound(dq2, gdk_ref))
    ones3 = jnp.where(rowid < 3, 1.0, 0.0)
    zeros8 = jnp.zeros((8, tm), F32)
    zeros40 = jnp.zeros((40, tm), F32)
    one_row16 = jnp.where(lax.broadcasted_iota(jnp.int32, (16, tm), 0) == 0, 1.0, 0.0)

    for h in range(FOX_HEADS):
        cs = slab(c_pieces, h)
        q_aug = jnp.concatenate([fq2[h * 64:(h + 1) * 64, :], cs, ones3, -slab(mf_pieces, h), zeros40], axis=0)
        k_aug = jnp.concatenate([fk2[h * 64:(h + 1) * 64, :], ones3, -cs, ones3, zeros40], axis=0)
        qfa_o[h * AUG_DK:(h + 1) * AUG_DK, :] = q_aug.astype(BF16)
        kfa_o[:, h * AUG_DK:(h + 1) * AUG_DK] = k_aug.T.astype(BF16)
        v_aug = jnp.concatenate([fv[h * 64:(h + 1) * 64, :], one_row16], axis=0)
        vfa_o[h * FOX_DVA:(h + 1) * FOX_DVA, :] = v_aug.astype(BF16)

    for j in range(2 * DIFF_HEADS):
        q_aug = jnp.concatenate([dq2[j * 64:(j + 1) * 64, :], zeros8, zeros8, -slab(md_pieces, j), zeros40], axis=0)
        k_aug = jnp.concatenate([dk2[j * 64:(j + 1) * 64, :], zeros8, zeros8, ones3, zeros40], axis=0)
        qd_o[j * AUG_DK:(j + 1) * AUG_DK, :] = q_aug.astype(BF16)
        kd_o[:, j * AUG_DK:(j + 1) * AUG_DK] = k_aug.T.astype(BF16)

    for h in range(DIFF_HEADS):
        v_aug = jnp.concatenate([dv_t[h * 128:(h + 1) * 128, :], one_row16], axis=0)
        vda_o[h * DIFF_DVA:(h + 1) * DIFF_DVA, :] = v_aug.astype(BF16)


def _project(x2d, params, is_prompt, tm, pos_base, pos_tile, pos_lane):
    n_tok = x2d.shape[0]
    n_tiles = n_tok // tm
    full = lambda a: pl.BlockSpec(a.shape, lambda i: (0,) * a.ndim)
    in_specs = [pl.BlockSpec((tm, D_MODEL), lambda i: (i, 0))] + [full(p) for p in params]
    t_spec = lambda rows: pl.BlockSpec((rows, tm), lambda i: (0, i))
    r_spec = lambda cols: pl.BlockSpec((tm, cols), lambda i: (i, 0))
    if is_prompt:
        sds = jax.ShapeDtypeStruct
        out_shape = [
            sds((512, n_tok), F32), sds((512, n_tok), F32), sds((8, n_tok), F32), sds((512, n_tok), F32),
            sds((DIFF_HEADS * n_tok, DIFF_V_DIM), F32),
            sds((8 * AUG_DK, n_tok), BF16), sds((n_tok, 8 * AUG_DK), BF16),
            sds((FOX_HEADS * FOX_DVA, n_tok), BF16), sds((512, n_tok), BF16),
            sds((8 * AUG_DK, n_tok), BF16), sds((n_tok, 8 * AUG_DK), BF16),
            sds((DIFF_HEADS * DIFF_DVA, n_tok), BF16), sds((512, n_tok), BF16),
        ]
        out_specs = [
            t_spec(512), t_spec(512), t_spec(8), t_spec(512),
            pl.BlockSpec((DIFF_HEADS * tm, DIFF_V_DIM), lambda i: (i, 0)),
            t_spec(8 * AUG_DK), r_spec(8 * AUG_DK),
            t_spec(FOX_HEADS * FOX_DVA), t_spec(512),
            t_spec(8 * AUG_DK), r_spec(8 * AUG_DK), t_spec(DIFF_HEADS * DIFF_DVA), t_spec(512),
        ]
        scratch = [pltpu.VMEM((8, 128), F32)]
    else:
        sds = jax.ShapeDtypeStruct
        s3 = lambda rows: pl.BlockSpec((None, rows, tm), lambda i: (i, 0, 0))
        out_shape = [
            sds((n_tiles, 512, tm), F32), sds((n_tiles, 512, tm), F32), sds((n_tiles, 8, tm), F32),
            sds((n_tiles, 512, tm), F32), sds((n_tiles, tm, 512), F32),
            sds((n_tiles, 512, tm), F32), sds((n_tiles, 512, tm), F32),
            sds((n_tiles, 512, tm), F32), sds((n_tiles, 512, tm), F32),
        ]
        out_specs = [s3(512), s3(512), s3(8), s3(512),
                     pl.BlockSpec((None, tm, 512), lambda i: (i, 0, 0)),
                     s3(512), s3(512), s3(512), s3(512)]
        scratch = []
    return pl.pallas_call(
        functools.partial(_proj_body, is_prompt, tm, pos_base, pos_tile, pos_lane),
        grid=(n_tiles,),
        in_specs=in_specs,
        out_specs=out_specs,
        out_shape=out_shape,
        scratch_shapes=scratch,
        compiler_params=pltpu.CompilerParams(dimension_semantics=("arbitrary",),
                                             vmem_limit_bytes=VMEM_LIMIT),
        name="proj_prompt" if is_prompt else "proj_sample",
    )(x2d, *params)


def _attn_parts(is_diff, tq, tk, dva, qi, ki, in_refs, og_ref, m_s, acc_s):
    qT_ref, k_ref, v_ref, gz_ref = in_refs[:4]
    if is_diff:
        lam_ref, gs_ref = in_refs[4:]
    dk = AUG_DK

    def init():
        m_s[...] = jnp.full(m_s.shape, NEG, F32)
        acc_s[...] = jnp.zeros(acc_s.shape, F32)

    def update(masked, fixed):
        if fixed and not masked:
            for j in range(2):
                v0 = 0 if is_diff else j * dva
                for c in range(tq // PROMPT_COL_TILE):
                    cols = slice(c * PROMPT_COL_TILE, (c + 1) * PROMPT_COL_TILE)
                    for r in range(tk // PROMPT_ROW_TILE):
                        rows = slice(r * PROMPT_ROW_TILE, (r + 1) * PROMPT_ROW_TILE)
                        s = jnp.dot(k_ref[rows, j * dk:(j + 1) * dk], qT_ref[j * dk:(j + 1) * dk, cols],
                                    preferred_element_type=F32)
                        acc_s[j, :, cols] += jnp.dot(v_ref[v0:v0 + dva, rows], jnp.exp2(s).astype(BF16),
                                                     preferred_element_type=F32)
            return
        for j in range(2):
            s = jnp.dot(k_ref[:, j * dk:(j + 1) * dk], qT_ref[j * dk:(j + 1) * dk, :],
                        preferred_element_type=F32)
            if masked:
                kpos = ki * tk + lax.broadcasted_iota(jnp.int32, (tk, tq), 0)
                qpos = qi * tq + lax.broadcasted_iota(jnp.int32, (tk, tq), 1)
                s = jnp.where(kpos <= qpos, s, NEG)
            vj = v_ref[...] if is_diff else v_ref[j * dva:(j + 1) * dva, :]
            if fixed:
                acc_s[j] += jnp.dot(vj, jnp.exp2(s).astype(BF16), preferred_element_type=F32)
            else:
                m_old = m_s[j]
                m_new = jnp.maximum(m_old, jnp.max(s, axis=0, keepdims=True))
                alpha = jnp.exp2(m_old - m_new)
                pb = jnp.exp2(s - m_new).astype(BF16)
                acc_s[j] = alpha * acc_s[j] + jnp.dot(vj, pb, preferred_element_type=F32)
                m_s[j] = m_new

    def finalize():
        if is_diff:
            lam = _diff_lambda(lam_ref)
            a1 = acc_s[0]
            a2 = acc_s[1]
            od = a1[0:128, :] / a1[128:129, :] - lam * (a2[0:128, :] / a2[128:129, :])
            ms = jnp.mean(od * od, axis=0, keepdims=True)
            y = (od * lax.rsqrt(ms + NORM_EPS)) * gs_ref[...] * (1.0 - LAMBDA_INIT)
            og_ref[...] = (y * gz_ref[...].astype(F32)).astype(BF16)
        else:
            for j in range(2):
                a = acc_s[j]
                o = a[0:64, :] / a[64:65, :]
                og_ref[j * 64:(j + 1) * 64, :] = (o * gz_ref[j * 64:(j + 1) * 64, :].astype(F32)).astype(BF16)

    return init, update, finalize


def _causal_pairs(n_tok, tq, tk):
    qi_l, ki_l = [], []
    for qi in range(n_tok // tq):
        for ki in range(((qi + 1) * tq - 1) // tk + 1):
            qi_l.append(qi)
            ki_l.append(ki)
    return jnp.asarray(qi_l, jnp.int32), jnp.asarray(ki_l, jnp.int32)


def _fused_body(is_diff, tq, tk, dva, pg, ng, n_pairs, n_steps, n_sample_steps, pt_base, n_prompt_in, *refs):
    qi_tab, ki_tab, fixed_tab, pt_ref = refs[:4]
    refs = refs[4:]
    n_sample_in = N_SAMPLE_SEQ_INPUTS + 2
    prompt_in = refs[:n_prompt_in]
    pools = refs[n_prompt_in:n_prompt_in + 5]
    sample_in = refs[n_prompt_in + 5:n_prompt_in + 5 + n_sample_in]
    og_ref, ogs_ref = refs[n_prompt_in + 5 + n_sample_in:n_prompt_in + 5 + n_sample_in + 2]
    scratch = refs[n_prompt_in + 5 + n_sample_in + 2:]
    m_s, acc_s = scratch[:2]
    page_bufs = scratch[2:7]
    page_sem = scratch[7]
    p = pl.program_id(1)
    step = pl.program_id(0) * n_steps + p
    pp = jnp.minimum(p, n_pairs - 1)
    qi = qi_tab[pp]
    ki = ki_tab[pp]
    g = lax.rem(step, ng)
    slot = lax.rem(step, 2)
    a_init, a_update, a_finalize = _attn_parts(is_diff, tq, tk, dva, qi, ki, prompt_in, og_ref, m_s, acc_s)
    s_clear, s_init, s_pages, s_last_pages = _sample_parts(pg, slot, page_bufs, sample_in, ogs_ref, scratch[8:])
    both = jnp.logical_and
    no = jnp.logical_not

    has_prompt = p < n_pairs
    has_sample = step < n_sample_steps
    fixed_shift = fixed_tab[0] == 1
    need_mask = (ki + 1) * tk - 1 > qi * tq

    def start_pages(s, to_slot):
        for c in _page_copies(pg, pools, page_bufs, page_sem, to_slot, lambda i: pt_ref[pt_base + s * pg + i]):
            c.start()

    pl.when(step == 0)(lambda: start_pages(0, 0))
    pl.when(step + 1 < n_sample_steps)(lambda: start_pages(step + 1, 1 - slot))

    @pl.when(has_sample)
    def _():
        for c in _page_copies(pg, pools, page_bufs, page_sem, slot, lambda i: 0):
            c.wait()

    pl.when(step == 0)(s_clear)
    pl.when(both(has_prompt, ki == 0))(a_init)
    pl.when(both(has_sample, g == 0))(s_init)

    paired = both(both(has_prompt, has_sample), fixed_shift)
    last_step = g == ng - 1

    for masked in (False, True):
        for is_last in (False, True):
            c_mask = need_mask if masked else no(need_mask)
            c_last = last_step if is_last else no(last_step)

            def paired_step(masked=masked, is_last=is_last):
                a_update(masked, True)
                (s_last_pages if is_last else s_pages)()

            pl.when(both(paired, both(c_mask, c_last)))(paired_step)

    @pl.when(no(paired))
    def _():
        for masked in (True, False):
            for fixed in (True, False):
                c_mask = need_mask if masked else no(need_mask)
                c_fixed = fixed_shift if fixed else no(fixed_shift)
                pl.when(both(has_prompt, both(c_mask, c_fixed)))(functools.partial(a_update, masked, fixed))
        pl.when(both(has_sample, no(last_step)))(s_pages)
        pl.when(both(has_sample, last_step))(s_last_pages)

    pl.when(both(has_prompt, ki == ((qi + 1) * tq - 1) // tk))(a_finalize)


def _fused_attention(is_diff, qT, k, vT, gzT, g_q, g_k, prompt_extra, n_tok, tq, tk,
                     page_table, seq0, n_seq_call, pools, per_seq, sample_extra, pg):
    dva = DIFF_DVA if is_diff else FOX_DVA
    v_rows = dva if is_diff else 2 * dva
    qi_tab, ki_tab = _causal_pairs(n_tok, tq, tk)
    n_pairs = qi_tab.shape[0]
    n_pages = page_table.shape[1]
    ng = n_pages // pg
    n_sample_steps = n_seq_call * ng
    n_steps = max(n_pairs, -(-n_sample_steps // 4))
    bound = QK_SCALE * HEAD_DIM * jnp.max(jnp.abs(g_q)) * jnp.max(jnp.abs(g_k))
    fixed_tab = (bound <= SHIFT_LIMIT).astype(jnp.int32).reshape(1)
    pt_flat = page_table.reshape(-1)

    def pmap(f):
        def index_map(h, p, qt, kt, fx, pt):
            pp = jnp.minimum(p, n_pairs - 1)
            return f(h, qt[pp], kt[pp])
        return index_map

    def sample_step(h, p):
        return jnp.minimum(h * n_steps + p, n_sample_steps - 1)

    def seq_map(h, p, qt, kt, fx, pt):
        return (seq0 + lax.div(sample_step(h, p), ng), 0, 0)

    def out_seq_map(h, p, qt, kt, fx, pt):
        return (lax.div(sample_step(h, p), ng), 0, 0)

    in_specs = [
        pl.BlockSpec((2 * AUG_DK, tq), pmap(lambda h, q, kk: (h, q))),
        pl.BlockSpec((tk, 2 * AUG_DK), pmap(lambda h, q, kk: (kk, h))),
        pl.BlockSpec((v_rows, tk), pmap(lambda h, q, kk: (h, kk))),
        pl.BlockSpec((128, tq), pmap(lambda h, q, kk: (h, q))),
    ] + [pl.BlockSpec(e.shape, pmap(lambda h, q, kk: (0, 0))) for e in prompt_extra]
    operands = [qT, k, vT, gzT, *prompt_extra]
    n_prompt_in = len(operands)
    for pool in pools:
        in_specs.append(pl.BlockSpec(memory_space=pl.ANY))
        operands.append(pool)
    assert len(per_seq) == N_SAMPLE_SEQ_INPUTS
    for a in per_seq:
        in_specs.append(pl.BlockSpec((1,) + a.shape[1:], seq_map))
        operands.append(a)
    for e in sample_extra:
        in_specs.append(pl.BlockSpec(e.shape, pmap(lambda h, q, kk: (0, 0))))
        operands.append(e)
    scratch = [
        pltpu.VMEM((2, 1, tq), F32), pltpu.VMEM((2, dva, tq), F32),
        pltpu.VMEM((2, pg, 512, 128), F32), pltpu.VMEM((2, pg, 512, 128), F32),
        pltpu.VMEM((2, pg, 8, 128), F32),
        pltpu.VMEM((2, pg, 512, 128), F32), pltpu.VMEM((2, pg, 512, 128), F32),
        pltpu.SemaphoreType.DMA((2,)),
        pltpu.VMEM((64, 512), BF16), pltpu.VMEM((64, 512), BF16),
        pltpu.VMEM((64, 1), F32), pltpu.VMEM((64, 1), F32), pltpu.VMEM((64, 512), F32),
        pltpu.VMEM((64, 1), F32), pltpu.VMEM((64, 1), F32), pltpu.VMEM((64, 128), F32),
        pltpu.VMEM((8, 128), F32),
        pltpu.VMEM((512, 128), F32), pltpu.VMEM((512, 128), F32), pltpu.VMEM((8, 128), F32),
        pltpu.VMEM((512, 128), F32), pltpu.VMEM((512, 128), F32),
    ]
    assert len(scratch) == 8 + N_SAMPLE_SCRATCH
    grid_spec = pltpu.PrefetchScalarGridSpec(
        num_scalar_prefetch=4,
        grid=(4, n_steps),
        in_specs=in_specs,
        out_specs=[pl.BlockSpec((128, tq), pmap(lambda h, q, kk: (h, q))),
                   pl.BlockSpec((1, 8, 1024), out_seq_map)],
        scratch_shapes=scratch,
    )
    return pl.pallas_call(
        functools.partial(_fused_body, is_diff, tq, tk, dva, pg, ng, n_pairs, n_steps,
                          n_sample_steps, seq0 * n_pages, n_prompt_in),
        grid_spec=grid_spec,
        out_shape=[jax.ShapeDtypeStruct((512, n_tok), BF16),
                   jax.ShapeDtypeStruct((n_seq_call, 8, 1024), BF16)],
        compiler_params=pltpu.CompilerParams(dimension_semantics=("arbitrary", "arbitrary"),
                                             vmem_limit_bytes=VMEM_LIMIT),
        name="attn_diff" if is_diff else "attn_fox",
    )(qi_tab, ki_tab, fixed_tab, pt_flat, *operands)


N_SAMPLE_SEQ_INPUTS = 4
N_SAMPLE_SCRATCH = 14


def _page_copies(pg, pools, bufs, sem, slot, page_of):
    return [pltpu.make_async_copy(pool.at[page_of(i)], buf.at[slot, i], sem.at[slot])
            for i in range(pg) for pool, buf in zip(pools, bufs)]


def _sample_parts(pg, slot, page_bufs, in_refs, og_ref, scratch):
    fk_b, fv_b, lf_b, dk_b, dv_b = page_bufs
    tok_ref, newk_ref, lfn_ref, vdn_ref, lam_ref, gs_ref = in_refs
    (qbf_s, qbd_s, mf_s, lf_s, accf_s, md_s, ld_s, accd_s, off_s,
     nk_s, nv_s, nl_s, ndk_s, ndv_s) = scratch

    def blockdiag():
        row = lax.broadcasted_iota(jnp.int32, (64, 512), 0)
        col = lax.broadcasted_iota(jnp.int32, (64, 512), 1)
        return (row >> 3) == (col >> 6)

    def init():
        qf = jnp.concatenate([tok_ref[0, :, 0:512]] * 8, axis=0)
        qd = jnp.concatenate([tok_ref[0, :, 512:1024]] * 8, axis=0)
        qbf_s[...] = jnp.where(blockdiag(), qf, 0.0).astype(BF16)
        qbd_s[...] = jnp.where(blockdiag(), qd, 0.0).astype(BF16)
        mf_s[...] = jnp.full(mf_s.shape, NEG, F32)
        md_s[...] = jnp.full(md_s.shape, NEG, F32)
        lf_s[...] = jnp.zeros(lf_s.shape, F32)
        ld_s[...] = jnp.zeros(ld_s.shape, F32)
        accf_s[...] = jnp.zeros(accf_s.shape, F32)
        accd_s[...] = jnp.zeros(accd_s.shape, F32)
        off_s[...] = jnp.zeros(off_s.shape, F32)

    def attend(pages, mask):
        n = len(pages)
        cat = lambda xs, axis: xs[0] if n == 1 else jnp.concatenate(xs, axis=axis)
        cw = _lane_cumsum(cat([pg_[2] for pg_ in pages], 0))
        off = off_s[:, 0:1]
        cs = []
        for i in range(n):
            cwi = cw[i * 8:(i + 1) * 8, :]
            c = (off + cwi) * LOG2E
            cs.append(jnp.broadcast_to(c[:, None, :], (8, 8, 128)).reshape(64, 128))
            off = off + cwi[:, 127:128]
        off_s[...] = jnp.broadcast_to(off, off_s.shape)

        kf = cat([pg_[0].astype(BF16) for pg_ in pages], 1)
        s_f = jnp.dot(qbf_s[...], kf, preferred_element_type=F32) - cat(cs, 1)
        kd = cat([pg_[3].astype(BF16) for pg_ in pages], 1)
        s_d = jnp.dot(qbd_s[...], kd, preferred_element_type=F32)
        if mask is not None:
            s_f = jnp.where(mask, s_f, NEG)
            s_d = jnp.where(mask, s_d, NEG)

        m_old = mf_s[...]
        m_new = jnp.maximum(m_old, jnp.max(s_f, axis=1, keepdims=True))
        alpha = jnp.exp2(m_old - m_new)
        pr = jnp.exp2(s_f - m_new)
        lf_s[...] = alpha * lf_s[...] + jnp.sum(pr, axis=1, keepdims=True)
        mf_s[...] = m_new
        vf = cat([pg_[1].astype(BF16) for pg_ in pages], 1)
        pv = lax.dot_general(pr.astype(BF16), vf, (((1,), (1,)), ((), ())), preferred_element_type=F32)
        accf_s[...] = alpha * accf_s[...] + pv

        m_old = md_s[...]
        m_new = jnp.maximum(m_old, jnp.max(s_d, axis=1, keepdims=True))
        alpha = jnp.exp2(m_old - m_new)
        pr = jnp.exp2(s_d - m_new)
        ld_s[...] = alpha * ld_s[...] + jnp.sum(pr, axis=1, keepdims=True)
        md_s[...] = m_new
        pb = pr.astype(BF16)
        for h in range(DIFF_HEADS):
            rows = slice(h * 16, (h + 1) * 16)
            vh = cat([pg_[4][pl.ds(h, 128, stride=4), :].astype(BF16) for pg_ in pages], 0)
            pv = jnp.dot(pb[rows, :], vh, preferred_element_type=F32)
            accd_s[rows, :] = alpha[rows, :] * accd_s[rows, :] + pv

    def cache_pages():
        return [(fk_b[slot, i], fv_b[slot, i], lf_b[slot, i], dk_b[slot, i], dv_b.at[slot, i])
                for i in range(pg)]

    def pages():
        attend(cache_pages(), None)

    def clear_new_page():
        nk_s[...] = jnp.zeros(nk_s.shape, F32)
        nv_s[...] = jnp.zeros(nv_s.shape, F32)
        nl_s[...] = jnp.zeros(nl_s.shape, F32)
        ndk_s[...] = jnp.zeros(ndk_s.shape, F32)
        ndv_s[...] = jnp.zeros(ndv_s.shape, F32)

    def last_pages():
        nk_s[:, 0:8] = newk_ref[0, 0:512, :]
        nv_s[:, 0:8] = newk_ref[0, 512:1024, :]
        nl_s[:, 0:8] = lfn_ref[0]
        ndk_s[:, 0:8] = newk_ref[0, 1024:1536, :]
        ndv_s[0:32, :] = vdn_ref[0]
        w = (pg + 1) * 128
        r = lax.broadcasted_iota(jnp.int32, (64, w), 0)
        c = lax.broadcasted_iota(jnp.int32, (64, w), 1)
        attend(cache_pages() + [(nk_s[...], nv_s[...], nl_s[...], ndk_s[...], ndv_s)],
               (c - pg * 128) <= (r & 7))

        of = jnp.where(blockdiag(), accf_s[...] / lf_s[...], 0.0)
        of = jnp.sum(of.reshape(8, 8, 512), axis=0)
        ogf = of * tok_ref[0, :, 1024:1536]
        lam = _diff_lambda(lam_ref)
        on = accd_s[...] / ld_s[...]
        outs = [ogf]
        for h in range(DIFF_HEADS):
            od = on[h * 16:h * 16 + 8, :] - lam * on[h * 16 + 8:h * 16 + 16, :]
            ms = jnp.mean(od * od, axis=-1, keepdims=True)
            y = (od * lax.rsqrt(ms + NORM_EPS)) * gs_ref[...] * (1.0 - LAMBDA_INIT)
            outs.append(y * tok_ref[0, :, 1536 + h * 128:1536 + (h + 1) * 128])
        og_ref[0] = jnp.concatenate(outs, axis=1).astype(BF16)

    return clear_new_page, init, pages, last_pages


def _out_t_body(x_ref, of_ref, od_ref, woT_ref, o_ref):
    y_t = jnp.dot(woT_ref[:, 0:512], of_ref[...], preferred_element_type=F32)
    y_t = y_t + jnp.dot(woT_ref[:, 512:1024], od_ref[...], preferred_element_type=F32)
    o_ref[...] = x_ref[...] + y_t.T


def _out_proj_t(x2d, ogf_t, ogd_t, wo_t, tm):
    n_tok = x2d.shape[0]
    return pl.pallas_call(
        _out_t_body,
        grid=(n_tok // tm,),
        in_specs=[pl.BlockSpec((tm, D_MODEL), lambda i: (i, 0)),
                  pl.BlockSpec((512, tm), lambda i: (0, i)),
                  pl.BlockSpec((512, tm), lambda i: (0, i)),
                  pl.BlockSpec((D_MODEL, D_MODEL), lambda i: (0, 0))],
        out_specs=pl.BlockSpec((tm, D_MODEL), lambda i: (i, 0)),
        out_shape=jax.ShapeDtypeStruct((n_tok, D_MODEL), F32),
        compiler_params=pltpu.CompilerParams(dimension_semantics=("arbitrary",),
                                             vmem_limit_bytes=VMEM_LIMIT),
        name="out_proj_prompt",
    )(x2d, ogf_t, ogd_t, wo_t)


def _out_body(x_ref, og_ref, wo_ref, o_ref):
    o_ref[...] = x_ref[...] + jnp.dot(og_ref[...], wo_ref[...], preferred_element_type=F32)


def _out_proj(x2d, og, wo, tm):
    n_tok = x2d.shape[0]
    return pl.pallas_call(
        _out_body,
        grid=(n_tok // tm,),
        in_specs=[pl.BlockSpec((tm, D_MODEL), lambda i: (i, 0)),
                  pl.BlockSpec((tm, D_MODEL), lambda i: (i, 0)),
                  pl.BlockSpec((D_MODEL, D_MODEL), lambda i: (0, 0))],
        out_specs=pl.BlockSpec((tm, D_MODEL), lambda i: (i, 0)),
        out_shape=jax.ShapeDtypeStruct((n_tok, D_MODEL), F32),
        compiler_params=pltpu.CompilerParams(dimension_semantics=("arbitrary",),
                                             vmem_limit_bytes=VMEM_LIMIT),
        name="out_proj_sample",
    )(x2d, og, wo)


def kernel(x_prompt, x_sample, cache_fox_k, cache_fox_v, cache_fox_logf, cache_diff_k, cache_diff_v,
           page_table, g_norm, w_in, b_f, g_fox_q, g_fox_k, g_diff_q, g_diff_k,
           lam_q1, lam_k1, lam_q2, lam_k2, g_subln, w_out):
    seq = x_prompt.shape[1]
    n_seq, dec = x_sample.shape[0], x_sample.shape[1]
    past = page_table.shape[1] * PAGE_SIZE
    n_pool = cache_fox_k.shape[1]

    w = w_in[0]
    sizes = [FOX_WIDTH, FOX_WIDTH, FOX_WIDTH, FOX_HEADS, FOX_WIDTH, DIFF_QK_WIDTH, DIFF_QK_WIDTH,
             DIFF_WIDTH, DIFF_WIDTH]
    offs = [0]
    for s in sizes:
        offs.append(offs[-1] + s)
    col = lambda k: w[:, offs[k]:offs[k + 1]]
    w_t = jnp.concatenate(
        [col(0), col(1), col(2), col(4), col(5), col(6), col(7), col(8), col(3),
         jnp.zeros((D_MODEL, 8), w.dtype)], axis=1).T.astype(BF16)
    w_dv = col(7).astype(BF16)
    half = ROT_DIM // 2
    inv = (ROPE_THETA ** (-jnp.arange(half, dtype=F32) * 2.0 / ROT_DIM)).reshape(half, 1)
    colv = lambda a: a[0].astype(F32).reshape(-1, 1)
    params = (g_norm.astype(F32), w_t, w_dv, colv(b_f), colv(g_fox_q), colv(g_fox_k),
              colv(g_diff_q), colv(g_diff_k), inv)
    lam_params = (jnp.concatenate([lam_q1, lam_k1, lam_q2, lam_k2], axis=0).astype(F32),)
    wo = w_out[0].astype(BF16)

    xp2 = x_prompt[0]
    (fk_t, fv_t, lf_t, dk_t, dv_p, qfa_t, kfa, vfa_t, gzf_t, qd_t, kd, vda_t, gzd_t) = _project(
        xp2, params, True, 256, 0, 256, 1)

    xs_tb = jnp.transpose(x_sample, (1, 0, 2)).reshape(dec * n_seq, D_MODEL)
    (sfk, sfv, slf, sdk, sdv, sfq, sdq, sgzf, sgzd) = _project(
        xs_tb, params, False, n_seq, past, 1, 0)
    bt = lambda a: jnp.transpose(a, (2, 0, 1))
    bf = lambda a: jnp.transpose(a, (2, 1, 0))
    per_seq = (bt(jnp.concatenate([sfq, sdq, sgzf, sgzd], axis=1)),
               bf(jnp.concatenate([sfk, sfv, sdk], axis=1)),
               bf(slf),
               jnp.transpose(sdv, (1, 0, 2)).reshape(n_seq, dec * DIFF_HEADS, DIFF_V_DIM))
    pools = (
        jnp.transpose(cache_fox_k[0], (0, 2, 3, 1)).reshape(n_pool, 512, PAGE_SIZE),
        jnp.transpose(cache_fox_v[0], (0, 2, 3, 1)).reshape(n_pool, 512, PAGE_SIZE),
        jnp.transpose(cache_fox_logf[0], (0, 2, 1)),
        jnp.transpose(cache_diff_k[0], (0, 2, 3, 4, 1)).reshape(n_pool, 512, PAGE_SIZE),
        cache_diff_v[0].reshape(n_pool, PAGE_SIZE * DIFF_HEADS, DIFF_V_DIM),
    )

    tq = tk = 1024
    n_a = (n_seq + 1) // 2
    sample_extra = lam_params + (g_subln.astype(F32),)
    ogf_t, og_sa = _fused_attention(False, qfa_t, kfa, vfa_t, gzf_t, g_fox_q, g_fox_k, (), seq, tq, tk,
                                    page_table, 0, n_a, pools, per_seq, sample_extra, SAMPLE_PAGES_PER_STEP)
    ogd_t, og_sb = _fused_attention(True, qd_t, kd, vda_t, gzd_t, g_diff_q, g_diff_k,
                                    lam_params + (colv(g_subln),), seq, tq, tk,
                                    page_table, n_a, n_seq - n_a, pools, per_seq, sample_extra,
                                    SAMPLE_PAGES_PER_STEP)
    og_s = jnp.concatenate([og_sa, og_sb], axis=0)
    y_prompt = _out_proj_t(xp2, ogf_t, ogd_t, wo.T, 512)[None]
    y_sample = _out_proj(x_sample.reshape(n_seq * dec, D_MODEL), og_s.reshape(n_seq * dec, D_MODEL),
                         wo, 256).reshape(n_seq, dec, D_MODEL)

    new_fk_p = jnp.transpose(fk_t.reshape(FOX_HEADS, HEAD_DIM, seq), (2, 0, 1))[None, None]
    new_fv_p = jnp.transpose(fv_t.reshape(FOX_HEADS, HEAD_DIM, seq), (2, 0, 1))[None, None]
    new_fl_p = jnp.transpose(lf_t, (1, 0))[None, None]
    new_dk_p = jnp.transpose(dk_t.reshape(DIFF_HEADS, 2, DIFF_QK_DIM, seq), (3, 0, 1, 2))[None, None]
    new_dv_p = dv_p.reshape(seq, DIFF_HEADS, DIFF_V_DIM)[None, None]
    new_fk_s = jnp.transpose(sfk.reshape(dec, FOX_HEADS, HEAD_DIM, n_seq), (3, 0, 1, 2))[None]
    new_fv_s = jnp.transpose(sfv.reshape(dec, FOX_HEADS, HEAD_DIM, n_seq), (3, 0, 1, 2))[None]
    new_fl_s = jnp.transpose(slf, (2, 0, 1))[None]
    new_dk_s = jnp.transpose(sdk.reshape(dec, DIFF_HEADS, 2, DIFF_QK_DIM, n_seq), (4, 0, 1, 2, 3))[None]
    new_dv_s = jnp.transpose(sdv, (1, 0, 2)).reshape(n_seq, dec, DIFF_HEADS, DIFF_V_DIM)[None]
    return (y_prompt, y_sample, new_fk_p, new_fv_p, new_fl_p, new_dk_p, new_dv_p,
            new_fk_s, new_fv_s, new_fl_s, new_dk_s, new_dv_s)
```
